```python
import math
import jax
import jax.numpy as jnp
from jax import lax
import numpy as np

D_MODEL = 1024
BATCH = 8
SEQ = 8192
DEPTH = 4

CHUNK = 64
SSM_WIDTH = D_MODEL // 2
SSM_GROUP = 16
SSM_GROUPS = SSM_WIDTH // SSM_GROUP
SSM_STATE = 64
ATTN_WIDTH = D_MODEL // 2
N_HEADS = 8
HEAD_DIM = ATTN_WIDTH // N_HEADS
Q_BLOCK = 128
EPS = 1e-6
DT_MIN = 0.001
DT_MAX = 0.1
IN_SIZES = (SSM_WIDTH, SSM_WIDTH, ATTN_WIDTH, ATTN_WIDTH, ATTN_WIDTH, ATTN_WIDTH, D_MODEL, D_MODEL)
IN_COLS = sum(IN_SIZES)
IN_SPLITS = tuple(int(s) for s in np.cumsum(IN_SIZES)[:-1])

kernel_name = "hybrid_s5_stickbreaking_gated_block"


def rms_norm(x, g):
    xf = x.astype(jnp.float32)
    y = xf * lax.rsqrt(jnp.mean(xf * xf, axis=-1, keepdims=True) + EPS)
    return (y * g.astype(jnp.float32)).astype(x.dtype)


def s5_branch(u, a_re, a_im, log_dt, b_re, b_im, c_re, c_im, d_skip, w_glu, b_glu):
    bsz, seqlen, _ = u.shape
    f32 = jnp.float32
    ug = u.reshape(bsz, seqlen, SSM_GROUPS, SSM_GROUP).astype(f32)
    a_re = a_re.astype(f32)
    a_im = a_im.astype(f32)
    dt = jnp.exp(log_dt.astype(f32))[:, None]
    mag = jnp.exp(a_re * dt)
    abar_re = mag * jnp.cos(a_im * dt)
    abar_im = mag * jnp.sin(a_im * dt)
    nr = abar_re - 1.0
    ni = abar_im
    den = a_re * a_re + a_im * a_im
    f_re = (nr * a_re + ni * a_im) / den
    f_im = (ni * a_re - nr * a_im) / den
    b_re = b_re.astype(f32)
    b_im = b_im.astype(f32)
    bb_re = f_re[..., None] * b_re - f_im[..., None] * b_im
    bb_im = f_re[..., None] * b_im + f_im[..., None] * b_re
    bu_re = jnp.einsum('blgh,gph->blgp', ug, bb_re)
    bu_im = jnp.einsum('blgh,gph->blgp', ug, bb_im)
    ar_t = jnp.broadcast_to(abar_re, bu_re.shape)
    ai_t = jnp.broadcast_to(abar_im, bu_re.shape)

    def combine(e1, e2):
        a1r, a1i, b1r, b1i = e1
        a2r, a2i, b2r, b2i = e2
        return (a1r * a2r - a1i * a2i,
                a1r * a2i + a1i * a2r,
                a2r * b1r - a2i * b1i + b2r,
                a2r * b1i + a2i * b1r + b2i)

    _, _, xr, xi = lax.associative_scan(combine, (ar_t, ai_t, bu_re, bu_im), axis=1)
    y = (jnp.einsum('blgp,ghp->blgh', xr, c_re.astype(f32))
         - jnp.einsum('blgp,ghp->blgh', xi, c_im.astype(f32)))
    y = y + d_skip.astype(f32).reshape(SSM_GROUPS, SSM_GROUP) * ug
    y = jax.nn.gelu(y.reshape(bsz, seqlen, SSM_WIDTH)).astype(u.dtype)
    gl = y @ w_glu + b_glu
    ga, gb = jnp.split(gl, 2, axis=-1)
    return ga * jax.nn.sigmoid(gb)


def stick_breaking_attention(q, k, v):
    seqlen = q.shape[1]
    scale = HEAD_DIM ** -0.5
    outs = []
    for i in range(seqlen // Q_BLOCK):
        q0 = i * Q_BLOCK
        kend = q0 + Q_BLOCK
        qb = q[:, q0:kend]
        kb = k[:, :kend]
        vb = v[:, :kend]
        z = jnp.einsum('bqhd,bkhd->bhqk', qb, kb).astype(jnp.float32) * scale
        qpos = q0 + jnp.arange(Q_BLOCK)[:, None]
        kpos = jnp.arange(kend)[None, :]
        mask = kpos < qpos
        log_beta = jax.nn.log_sigmoid(z)
        log_1m = jnp.where(mask, log_beta - z, 0.0)
        after = lax.cumsum(log_1m, axis=3, reverse=True) - log_1m
        w = jnp.where(mask, jnp.exp(log_beta + after), 0.0)
        outs.append(jnp.einsum('bhqk,bkhd->bqhd', w.astype(v.dtype), vb))
    return jnp.concatenate(outs, axis=1)


def _fwd_setup_inputs(seed: int = 0) -> dict:
    key = jax.random.key(seed)
    ks = jax.random.split(key, 20)
    f32 = jnp.float32
    nrm = lambda k, shape, s: jax.random.normal(k, shape, f32) * s
    x = jax.random.normal(ks[0], (BATCH, SEQ, D_MODEL), f32)
    pre_norm_g = 1.0 + nrm(ks[1], (DEPTH, D_MODEL), 0.02)
    post_norm_g = 1.0 + nrm(ks[2], (DEPTH, D_MODEL), 0.02)
    w_in = nrm(ks[3], (DEPTH, D_MODEL, IN_COLS), D_MODEL ** -0.5)
    ssm_a_re = -0.5 + nrm(ks[4], (DEPTH, SSM_GROUPS, SSM_STATE), 0.01)
    ssm_a_im = (math.pi * jnp.arange(SSM_STATE, dtype=f32))[None, None, :] + nrm(ks[5], (DEPTH, SSM_GROUPS, SSM_STATE), 0.01)
    ssm_log_dt = jax.random.uniform(ks[6], (DEPTH, SSM_GROUPS), f32, math.log(DT_MIN), math.log(DT_MAX))
    ssm_b_re = nrm(ks[7], (DEPTH, SSM_GROUPS, SSM_STATE, SSM_GROUP), (2 * SSM_GROUP) ** -0.5)
    ssm_b_im = nrm(ks[8], (DEPTH, SSM_GROUPS, SSM_STATE, SSM_GROUP), (2 * SSM_GROUP) ** -0.5)
    ssm_c_re = nrm(ks[9], (DEPTH, SSM_GROUPS, SSM_GROUP, SSM_STATE), (2 * SSM_STATE) ** -0.5)
    ssm_c_im = nrm(ks[10], (DEPTH, SSM_GROUPS, SSM_GROUP, SSM_STATE), (2 * SSM_STATE) ** -0.5)
    ssm_d = nrm(ks[11], (DEPTH, SSM_WIDTH), 1.0)
    w_glu = nrm(ks[12], (DEPTH, SSM_WIDTH, 2 * SSM_WIDTH), SSM_WIDTH ** -0.5)
    b_glu = nrm(ks[13], (DEPTH, 2 * SSM_WIDTH), 0.01)
    w_branch_ssm = nrm(ks[14], (DEPTH, SSM_WIDTH, D_MODEL), SSM_WIDTH ** -0.5)
    w_branch_attn = nrm(ks[15], (DEPTH, ATTN_WIDTH, D_MODEL), ATTN_WIDTH ** -0.5)
    w_out = nrm(ks[16], (DEPTH, D_MODEL, D_MODEL), D_MODEL ** -0.5)
    return {"x": x, "pre_norm_g": pre_norm_g, "post_norm_g": post_norm_g, "w_in": w_in,
            "ssm_a_re": ssm_a_re, "ssm_a_im": ssm_a_im, "ssm_log_dt": ssm_log_dt,
            "ssm_b_re": ssm_b_re, "ssm_b_im": ssm_b_im, "ssm_c_re": ssm_c_re, "ssm_c_im": ssm_c_im,
            "ssm_d": ssm_d, "w_glu": w_glu, "b_glu": b_glu,
            "w_branch_ssm": w_branch_ssm, "w_branch_attn": w_branch_attn, "w_out": w_out}


def _fwd_reference(x, pre_norm_g, post_norm_g, w_in, ssm_a_re, ssm_a_im, ssm_log_dt,
              ssm_b_re, ssm_b_im, ssm_c_re, ssm_c_im, ssm_d, w_glu, b_glu,
              w_branch_ssm, w_branch_attn, w_out):
    bsz, seqlen, _ = x.shape
    for l in range(DEPTH):
        h = rms_norm(x, pre_norm_g[l])
        proj = h @ w_in[l]
        u, z_ssm, q, k, v, z_attn, g_ssm, g_attn = jnp.split(proj, IN_SPLITS, axis=-1)
        y_s = s5_branch(u, ssm_a_re[l], ssm_a_im[l], ssm_log_dt[l], ssm_b_re[l], ssm_b_im[l],
                        ssm_c_re[l], ssm_c_im[l], ssm_d[l], w_glu[l], b_glu[l])
        y_s = y_s * jax.nn.silu(z_ssm)
        hs = (bsz, seqlen, N_HEADS, HEAD_DIM)
        y_a = stick_breaking_attention(q.reshape(hs), k.reshape(hs), v.reshape(hs))
        y_a = y_a.reshape(bsz, seqlen, ATTN_WIDTH) * jax.nn.silu(z_attn)
        merged = (jax.nn.sigmoid(g_ssm) * (y_s @ w_branch_ssm[l])
                  + jax.nn.sigmoid(g_attn) * (y_a @ w_branch_attn[l]))
        out = merged @ w_out[l]
        x = x + rms_norm(out, post_norm_g[l])
    return x


import jax as _jax
import jax.numpy as _jnp

TWIN_FORMAT = 'train_step'
FWD_PARAMS = ['x', 'pre_norm_g', 'post_norm_g', 'w_in', 'ssm_a_re', 'ssm_a_im', 'ssm_log_dt', 'ssm_b_re', 'ssm_b_im', 'ssm_c_re', 'ssm_c_im', 'ssm_d', 'w_glu', 'b_glu', 'w_branch_ssm', 'w_branch_attn', 'w_out']
TWIN_WEIGHTS = ['pre_norm_g', 'post_norm_g', 'w_in', 'ssm_a_re', 'ssm_a_im', 'ssm_log_dt', 'ssm_b_re', 'ssm_b_im', 'ssm_c_re', 'ssm_c_im', 'ssm_d', 'w_glu', 'b_glu', 'w_branch_ssm', 'w_branch_attn', 'w_out']
TWIN_DIFF_INPUT = 'x'
TWIN_INPUTS = ['x', 'pre_norm_g', 'post_norm_g', 'w_in', 'ssm_a_re', 'ssm_a_im', 'ssm_log_dt', 'ssm_b_re', 'ssm_b_im', 'ssm_c_re', 'ssm_c_im', 'ssm_d', 'w_glu', 'b_glu', 'w_branch_ssm', 'w_branch_attn', 'w_out', 'loss_target', 'm_pre_norm_g', 'm_post_norm_g', 'm_w_in', 'm_ssm_a_re', 'm_ssm_a_im', 'm_ssm_log_dt', 'm_ssm_b_re', 'm_ssm_b_im', 'm_ssm_c_re', 'm_ssm_c_im', 'm_ssm_d', 'm_w_glu', 'm_b_glu', 'm_w_branch_ssm', 'm_w_branch_attn', 'm_w_out', 'v_pre_norm_g', 'v_post_norm_g', 'v_w_in', 'v_ssm_a_re', 'v_ssm_a_im', 'v_ssm_log_dt', 'v_ssm_b_re', 'v_ssm_b_im', 'v_ssm_c_re', 'v_ssm_c_im', 'v_ssm_d', 'v_w_glu', 'v_b_glu', 'v_w_branch_ssm', 'v_w_branch_attn', 'v_w_out']
TWIN_OUTPUTS = ['loss', 'grad_x', 'grad_pre_norm_g', 'grad_post_norm_g', 'grad_w_in', 'grad_ssm_a_re', 'grad_ssm_a_im', 'grad_ssm_log_dt', 'grad_ssm_b_re', 'grad_ssm_b_im', 'grad_ssm_c_re', 'grad_ssm_c_im', 'grad_ssm_d', 'grad_w_glu', 'grad_b_glu', 'grad_w_branch_ssm', 'grad_w_branch_attn', 'grad_w_out', 'delta_pre_norm_g', 'delta_post_norm_g', 'delta_w_in', 'delta_ssm_a_re', 'delta_ssm_a_im', 'delta_ssm_log_dt', 'delta_ssm_b_re', 'delta_ssm_b_im', 'delta_ssm_c_re', 'delta_ssm_c_im', 'delta_ssm_d', 'delta_w_glu', 'delta_b_glu', 'delta_w_branch_ssm', 'delta_w_branch_attn', 'delta_w_out', 'new_m_pre_norm_g', 'new_m_post_norm_g', 'new_m_w_in', 'new_m_ssm_a_re', 'new_m_ssm_a_im', 'new_m_ssm_log_dt', 'new_m_ssm_b_re', 'new_m_ssm_b_im', 'new_m_ssm_c_re', 'new_m_ssm_c_im', 'new_m_ssm_d', 'new_m_w_glu', 'new_m_b_glu', 'new_m_w_branch_ssm', 'new_m_w_branch_attn', 'new_m_w_out', 'new_v_pre_norm_g', 'new_v_post_norm_g', 'new_v_w_in', 'new_v_ssm_a_re', 'new_v_ssm_a_im', 'new_v_ssm_log_dt', 'new_v_ssm_b_re', 'new_v_ssm_b_im', 'new_v_ssm_c_re', 'new_v_ssm_c_im', 'new_v_ssm_d', 'new_v_w_glu', 'new_v_b_glu', 'new_v_w_branch_ssm', 'new_v_w_branch_attn', 'new_v_w_out']
TWIN_LEAF_KINDS = {'loss': 'loss', 'grad_x': 'grad_x', 'grad_pre_norm_g': 'grad_w', 'grad_post_norm_g': 'grad_w', 'grad_w_in': 'grad_w', 'grad_ssm_a_re': 'grad_w', 'grad_ssm_a_im': 'grad_w', 'grad_ssm_log_dt': 'grad_w', 'grad_ssm_b_re': 'grad_w', 'grad_ssm_b_im': 'grad_w', 'grad_ssm_c_re': 'grad_w', 'grad_ssm_c_im': 'grad_w', 'grad_ssm_d': 'grad_w', 'grad_w_glu': 'grad_w', 'grad_b_glu': 'grad_w', 'grad_w_branch_ssm': 'grad_w', 'grad_w_branch_attn': 'grad_w', 'grad_w_out': 'grad_w', 'delta_pre_norm_g': 'delta_w', 'delta_post_norm_g': 'delta_w', 'delta_w_in': 'delta_w', 'delta_ssm_a_re': 'delta_w', 'delta_ssm_a_im': 'delta_w', 'delta_ssm_log_dt': 'delta_w', 'delta_ssm_b_re': 'delta_w', 'delta_ssm_b_im': 'delta_w', 'delta_ssm_c_re': 'delta_w', 'delta_ssm_c_im': 'delta_w', 'delta_ssm_d': 'delta_w', 'delta_w_glu': 'delta_w', 'delta_b_glu': 'delta_w', 'delta_w_branch_ssm': 'delta_w', 'delta_w_branch_attn': 'delta_w', 'delta_w_out': 'delta_w', 'new_m_pre_norm_g': 'new_m', 'new_m_post_norm_g': 'new_m', 'new_m_w_in': 'new_m', 'new_m_ssm_a_re': 'new_m', 'new_m_ssm_a_im': 'new_m', 'new_m_ssm_log_dt': 'new_m', 'new_m_ssm_b_re': 'new_m', 'new_m_ssm_b_im': 'new_m', 'new_m_ssm_c_re': 'new_m', 'new_m_ssm_c_im': 'new_m', 'new_m_ssm_d': 'new_m', 'new_m_w_glu': 'new_m', 'new_m_b_glu': 'new_m', 'new_m_w_branch_ssm': 'new_m', 'new_m_w_branch_attn': 'new_m', 'new_m_w_out': 'new_m', 'new_v_pre_norm_g': 'new_v', 'new_v_post_norm_g': 'new_v', 'new_v_w_in': 'new_v', 'new_v_ssm_a_re': 'new_v', 'new_v_ssm_a_im': 'new_v', 'new_v_ssm_log_dt': 'new_v', 'new_v_ssm_b_re': 'new_v', 'new_v_ssm_b_im': 'new_v', 'new_v_ssm_c_re': 'new_v', 'new_v_ssm_c_im': 'new_v', 'new_v_ssm_d': 'new_v', 'new_v_w_glu': 'new_v', 'new_v_b_glu': 'new_v', 'new_v_w_branch_ssm': 'new_v', 'new_v_w_branch_attn': 'new_v', 'new_v_w_out': 'new_v'}


def _forward(args):
    return _fwd_reference(*[args[k] for k in FWD_PARAMS])


def _output_shape():
    def fwd():
        inp = _fwd_setup_inputs(0)
        return _fwd_reference(*[inp[k] for k in FWD_PARAMS])
    out = _jax.eval_shape(fwd)
    return out.shape, out.dtype

N_MICROBATCH = 1
ADAM_LR = 0.001
ADAM_B1 = 0.9
ADAM_B2 = 0.999
ADAM_EPS = 1e-08
ADAM_WD = 0.01
ADAM_STEP = 10
PER_EXAMPLE_BATCH_AXIS = {'x': 0, 'loss_target': 0}
SHARED_INPUTS = []
_WEIGHT_DTYPES = {'pre_norm_g': _jnp.float32, 'post_norm_g': _jnp.float32, 'w_in': _jnp.float32, 'ssm_a_re': _jnp.float32, 'ssm_a_im': _jnp.float32, 'ssm_log_dt': _jnp.float32, 'ssm_b_re': _jnp.float32, 'ssm_b_im': _jnp.float32, 'ssm_c_re': _jnp.float32, 'ssm_c_im': _jnp.float32, 'ssm_d': _jnp.float32, 'w_glu': _jnp.float32, 'b_glu': _jnp.float32, 'w_branch_ssm': _jnp.float32, 'w_branch_attn': _jnp.float32, 'w_out': _jnp.float32}
MOMENT_SCALE = {'pre_norm_g': 1.664593e+00, 'post_norm_g': 6.386972e+01, 'w_in': 7.562399e-01, 'ssm_a_re': 4.051184e-02, 'ssm_a_im': 4.315234e-02, 'ssm_log_dt': 4.293861e+01, 'ssm_b_re': 2.612366e-02, 'ssm_b_im': 2.505170e-02, 'ssm_c_re': 5.144347e-02, 'ssm_c_im': 5.091150e-02, 'ssm_d': 1.127181e+00, 'w_glu': 7.839561e-01, 'b_glu': 2.216833e+00, 'w_branch_ssm': 7.693254e-01, 'w_branch_attn': 9.293321e-01, 'w_out': 1.214952e+00}


def _to_microbatches(a, axis):
    t = _jnp.moveaxis(a, axis, 0)
    t = t.reshape((N_MICROBATCH, t.shape[0] // N_MICROBATCH) + t.shape[1:])
    return _jnp.moveaxis(t, 1, axis + 1)


def setup_inputs(seed: int = 0) -> dict:
    inp = _fwd_setup_inputs(seed)
    key = _jax.random.fold_in(_jax.random.key(seed), 7919)
    shape, _ = _output_shape()
    out = dict(inp)
    out["loss_target"] = _jax.random.normal(_jax.random.fold_in(key, 0), shape, _jnp.float32)
    for i, name in enumerate(TWIN_WEIGHTS):
        w = inp[name].astype(_jnp.float32)
        if MOMENT_SCALE is None:
            s = _jnp.sqrt(_jnp.mean(_jnp.square(w)) + 1e-30)
        else:
            s = MOMENT_SCALE[name]
        km, kv = _jax.random.split(_jax.random.fold_in(key, i + 1))
        out[name] = w
        out["m_" + name] = s * _jax.random.normal(km, w.shape, _jnp.float32)
        out["v_" + name] = (s * s) * _jax.random.uniform(kv, w.shape, _jnp.float32, 0.5, 1.5)
    if N_MICROBATCH > 1:
        for name, axis in PER_EXAMPLE_BATCH_AXIS.items():
            out[name] = _to_microbatches(out[name], axis)
    return {'x': out['x'], 'pre_norm_g': out['pre_norm_g'], 'post_norm_g': out['post_norm_g'], 'w_in': out['w_in'], 'ssm_a_re': out['ssm_a_re'], 'ssm_a_im': out['ssm_a_im'], 'ssm_log_dt': out['ssm_log_dt'], 'ssm_b_re': out['ssm_b_re'], 'ssm_b_im': out['ssm_b_im'], 'ssm_c_re': out['ssm_c_re'], 'ssm_c_im': out['ssm_c_im'], 'ssm_d': out['ssm_d'], 'w_glu': out['w_glu'], 'b_glu': out['b_glu'], 'w_branch_ssm': out['w_branch_ssm'], 'w_branch_attn': out['w_branch_attn'], 'w_out': out['w_out'], 'loss_target': out['loss_target'], 'm_pre_norm_g': out['m_pre_norm_g'], 'm_post_norm_g': out['m_post_norm_g'], 'm_w_in': out['m_w_in'], 'm_ssm_a_re': out['m_ssm_a_re'], 'm_ssm_a_im': out['m_ssm_a_im'], 'm_ssm_log_dt': out['m_ssm_log_dt'], 'm_ssm_b_re': out['m_ssm_b_re'], 'm_ssm_b_im': out['m_ssm_b_im'], 'm_ssm_c_re': out['m_ssm_c_re'], 'm_ssm_c_im': out['m_ssm_c_im'], 'm_ssm_d': out['m_ssm_d'], 'm_w_glu': out['m_w_glu'], 'm_b_glu': out['m_b_glu'], 'm_w_branch_ssm': out['m_w_branch_ssm'], 'm_w_branch_attn': out['m_w_branch_attn'], 'm_w_out': out['m_w_out'], 'v_pre_norm_g': out['v_pre_norm_g'], 'v_post_norm_g': out['v_post_norm_g'], 'v_w_in': out['v_w_in'], 'v_ssm_a_re': out['v_ssm_a_re'], 'v_ssm_a_im': out['v_ssm_a_im'], 'v_ssm_log_dt': out['v_ssm_log_dt'], 'v_ssm_b_re': out['v_ssm_b_re'], 'v_ssm_b_im': out['v_ssm_b_im'], 'v_ssm_c_re': out['v_ssm_c_re'], 'v_ssm_c_im': out['v_ssm_c_im'], 'v_ssm_d': out['v_ssm_d'], 'v_w_glu': out['v_w_glu'], 'v_b_glu': out['v_b_glu'], 'v_w_branch_ssm': out['v_w_branch_ssm'], 'v_w_branch_attn': out['v_w_branch_attn'], 'v_w_out': out['v_w_out']}


def _loss(weights, diff, rest, loss_target):
    with _jax.named_scope("forward"):
        args = {**rest, TWIN_DIFF_INPUT: diff, **{k: w.astype(_WEIGHT_DTYPES[k]) for k, w in weights.items()}}
        y = _forward(args)
    with _jax.named_scope("loss_head"):
        err = _jnp.square(y.astype(_jnp.float32) - loss_target)
        return 0.5 * _jnp.sum(_jnp.mean(err, axis=-1)) if err.ndim else 0.5 * err


def _adamw(w, g, m, v):
    m = ADAM_B1 * m + (1.0 - ADAM_B1) * g
    v = ADAM_B2 * v + (1.0 - ADAM_B2) * _jnp.square(g)
    m_hat = m / (1.0 - ADAM_B1 ** ADAM_STEP)
    v_hat = v / (1.0 - ADAM_B2 ** ADAM_STEP)
    delta = -ADAM_LR * (m_hat / (_jnp.sqrt(v_hat) + ADAM_EPS) + ADAM_WD * w)
    return delta, m, v


def reference(x, pre_norm_g, post_norm_g, w_in, ssm_a_re, ssm_a_im, ssm_log_dt, ssm_b_re, ssm_b_im, ssm_c_re, ssm_c_im, ssm_d, w_glu, b_glu, w_branch_ssm, w_branch_attn, w_out, loss_target, m_pre_norm_g, m_post_norm_g, m_w_in, m_ssm_a_re, m_ssm_a_im, m_ssm_log_dt, m_ssm_b_re, m_ssm_b_im, m_ssm_c_re, m_ssm_c_im, m_ssm_d, m_w_glu, m_b_glu, m_w_branch_ssm, m_w_branch_attn, m_w_out, v_pre_norm_g, v_post_norm_g, v_w_in, v_ssm_a_re, v_ssm_a_im, v_ssm_log_dt, v_ssm_b_re, v_ssm_b_im, v_ssm_c_re, v_ssm_c_im, v_ssm_d, v_w_glu, v_b_glu, v_w_branch_ssm, v_w_branch_attn, v_w_out):
    given = dict(x=x, pre_norm_g=pre_norm_g, post_norm_g=post_norm_g, w_in=w_in, ssm_a_re=ssm_a_re, ssm_a_im=ssm_a_im, ssm_log_dt=ssm_log_dt, ssm_b_re=ssm_b_re, ssm_b_im=ssm_b_im, ssm_c_re=ssm_c_re, ssm_c_im=ssm_c_im, ssm_d=ssm_d, w_glu=w_glu, b_glu=b_glu, w_branch_ssm=w_branch_ssm, w_branch_attn=w_branch_attn, w_out=w_out, loss_target=loss_target, m_pre_norm_g=m_pre_norm_g, m_post_norm_g=m_post_norm_g, m_w_in=m_w_in, m_ssm_a_re=m_ssm_a_re, m_ssm_a_im=m_ssm_a_im, m_ssm_log_dt=m_ssm_log_dt, m_ssm_b_re=m_ssm_b_re, m_ssm_b_im=m_ssm_b_im, m_ssm_c_re=m_ssm_c_re, m_ssm_c_im=m_ssm_c_im, m_ssm_d=m_ssm_d, m_w_glu=m_w_glu, m_b_glu=m_b_glu, m_w_branch_ssm=m_w_branch_ssm, m_w_branch_attn=m_w_branch_attn, m_w_out=m_w_out, v_pre_norm_g=v_pre_norm_g, v_post_norm_g=v_post_norm_g, v_w_in=v_w_in, v_ssm_a_re=v_ssm_a_re, v_ssm_a_im=v_ssm_a_im, v_ssm_log_dt=v_ssm_log_dt, v_ssm_b_re=v_ssm_b_re, v_ssm_b_im=v_ssm_b_im, v_ssm_c_re=v_ssm_c_re, v_ssm_c_im=v_ssm_c_im, v_ssm_d=v_ssm_d, v_w_glu=v_w_glu, v_b_glu=v_b_glu, v_w_branch_ssm=v_w_branch_ssm, v_w_branch_attn=v_w_branch_attn, v_w_out=v_w_out)
    weights = {n: given[n] for n in TWIN_WEIGHTS}
    shared = {n: given[n] for n in SHARED_INPUTS}
    per_example = {n: given[n] for n in ['x']}
    grad_fn = _jax.value_and_grad(_loss, argnums=(0, 1))

    def one_microbatch(ex, loss_target):
        ex = dict(ex)
        diff = ex.pop(TWIN_DIFF_INPUT)
        return grad_fn(weights, diff, {**shared, **ex}, loss_target)

    if N_MICROBATCH == 1:
        loss, (grad_w, grad_x) = one_microbatch(per_example, given["loss_target"])
    else:
        def body(carry, xs):
            loss_sum, grad_sum = carry
            l_k, (gw_k, gx_k) = one_microbatch(xs[0], xs[1])
            with _jax.named_scope("update"):
                return (loss_sum + l_k, _jax.tree.map(_jnp.add, grad_sum, gw_k)), gx_k

        init = (_jnp.zeros((), _jnp.float32), _jax.tree.map(_jnp.zeros_like, weights))
        (loss, grad_w), grad_x = _jax.lax.scan(body, init, (per_example, given["loss_target"]))
    with _jax.named_scope("update"):
        delta_w, new_m, new_v = {}, {}, {}
        for n in TWIN_WEIGHTS:
            delta_w[n], new_m[n], new_v[n] = _adamw(weights[n], grad_w[n], given["m_" + n], given["v_" + n])
    return (loss, grad_x, *[grad_w[n] for n in TWIN_WEIGHTS], *[delta_w[n] for n in TWIN_WEIGHTS],
            *[new_m[n] for n in TWIN_WEIGHTS], *[new_v[n] for n in TWIN_WEIGHTS])
```

```python
import jax
import jax.numpy as jnp
from jax import lax
from jax.experimental import pallas as pl
from jax.experimental.pallas import tpu as pltpu

F32 = jnp.float32
MXU_DTYPE = jnp.bfloat16

EPS = 1e-6
N_HEADS = 8
HEAD_DIM = 64
ADAM_LR = 0.001
ADAM_B1 = 0.9
ADAM_B2 = 0.999
ADAM_EPS = 1e-08
ADAM_WD = 0.01
ADAM_STEP = 10

N_DEV = 8
MESH_AXES = ("x", "y", "c")
VMEM_LIMIT_BYTES = 56 * 1024 * 1024
MESH = pl.DeviceIdType.MESH
ANY = pl.BlockSpec(memory_space=pl.ANY)
WHOLE_VMEM = pl.BlockSpec(memory_space=pltpu.VMEM)


def _params(n_grid_axes):
    return pltpu.CompilerParams(dimension_semantics=("arbitrary",) * n_grid_axes, vmem_limit_bytes=VMEM_LIMIT_BYTES)


def _dot(a, b):
    return jnp.dot(a.astype(MXU_DTYPE), b.astype(MXU_DTYPE), preferred_element_type=F32)


def _dot_nt(a, b):
    return lax.dot_general(a.astype(MXU_DTYPE), b.astype(MXU_DTYPE), (((1,), (1,)), ((), ())), preferred_element_type=F32)


def _dot_tn(a, b):
    return lax.dot_general(a.astype(MXU_DTYPE), b.astype(MXU_DTYPE), (((0,), (0,)), ((), ())), preferred_element_type=F32)


def _dot_split(a, tri):
    hi = a.astype(MXU_DTYPE)
    lo = (a - hi.astype(F32)).astype(MXU_DTYPE)
    return jnp.dot(hi, tri, preferred_element_type=F32) + jnp.dot(lo, tri, preferred_element_type=F32)


def _rms(v, g):
    return v * lax.rsqrt(jnp.mean(v * v, axis=-1, keepdims=True) + EPS) * g


def norm_proj(x, g, w, *, tl, tn, name):
    L, D = x.shape
    N = w.shape[1]

    def body(x_ref, g_ref, w_ref, o_ref, h_scr):
        @pl.when(pl.program_id(1) == 0)
        def _():
            h_scr[...] = _rms(x_ref[...], g_ref[...]).astype(MXU_DTYPE)

        o_ref[...] = jnp.dot(h_scr[...], w_ref[...], preferred_element_type=F32)

    return pl.pallas_call(
        body, name=name, grid=(L // tl, N // tn),
        in_specs=[pl.BlockSpec((tl, D), lambda i, j: (i, 0)), pl.BlockSpec((1, D), lambda i, j: (0, 0)),
                  pl.BlockSpec((D, tn), lambda i, j: (0, j))],
        out_specs=pl.BlockSpec((tl, tn), lambda i, j: (i, j)),
        out_shape=jax.ShapeDtypeStruct((L, N), F32),
        scratch_shapes=[pltpu.VMEM((tl, D), MXU_DTYPE)],
        compiler_params=_params(2),
    )(x, g, w)


def norm_proj_bwd_x(dproj, x, g, w, dxn, *, tl, name):
    L, D = x.shape
    N = w.shape[1]

    def body(dp_ref, x_ref, g_ref, w_ref, dxn_ref, dx_ref, dg_ref):
        @pl.when(pl.program_id(0) == 0)
        def _():
            dg_ref[...] = jnp.zeros_like(dg_ref)

        dh = _dot_nt(dp_ref[...], w_ref[...])
        _, vjp = jax.vjp(_rms, x_ref[...], g_ref[...])
        dx, dg = vjp(dh)
        dx_ref[...] = dxn_ref[...] + dx
        dg_ref[...] += dg

    return pl.pallas_call(
        body, name=name, grid=(L // tl,),
        in_specs=[pl.BlockSpec((tl, N), lambda i: (i, 0)), pl.BlockSpec((tl, D), lambda i: (i, 0)),
                  pl.BlockSpec((1, D), lambda i: (0, 0)), WHOLE_VMEM, pl.BlockSpec((tl, D), lambda i: (i, 0))],
        out_specs=[pl.BlockSpec((tl, D), lambda i: (i, 0)), pl.BlockSpec((1, D), lambda i: (0, 0))],
        out_shape=[jax.ShapeDtypeStruct((L, D), F32), jax.ShapeDtypeStruct((1, D), F32)],
        compiler_params=_params(1),
    )(dproj, x, g, w, dxn)


def norm_proj_bwd_w(dproj, x, g, *, tl, tn, name):
    L, D = x.shape
    N = dproj.shape[1]

    def body(dp_ref, x_ref, g_ref, dw_ref):
        @pl.when(pl.program_id(1) == 0)
        def _():
            dw_ref[...] = jnp.zeros_like(dw_ref)

        dw_ref[...] += _dot_tn(_rms(x_ref[...], g_ref[...]), dp_ref[...])

    return pl.pallas_call(
        body, name=name, grid=(N // tn, L // tl),
        in_specs=[pl.BlockSpec((tl, tn), lambda j, i: (i, j)), pl.BlockSpec((tl, D), lambda j, i: (i, 0)),
                  pl.BlockSpec((1, D), lambda j, i: (0, 0))],
        out_specs=pl.BlockSpec((None, D, tn), lambda j, i: (j, 0, 0)),
        out_shape=jax.ShapeDtypeStruct((N // tn, D, tn), F32),
        compiler_params=_params(2),
    )(dproj, x, g)


def _s5_disc(a_re, a_im, log_dt, b_re, b_im):
    dt = jnp.exp(log_dt)
    mag = jnp.exp(a_re * dt)
    abar_re = mag * jnp.cos(a_im * dt)
    abar_im = mag * jnp.sin(a_im * dt)
    nr = abar_re - 1.0
    ni = abar_im
    den = a_re * a_re + a_im * a_im
    f_re = (nr * a_re + ni * a_im) / den
    f_im = (ni * a_re - nr * a_im) / den
    bb_re = f_re[None] * b_re - f_im[None] * b_im
    bb_im = f_re[None] * b_im + f_im[None] * b_re
    return abar_re, abar_im, bb_re, bb_im


def s5_disc(a_re, a_im, log_dt, b_re, b_im, *, name):
    def body(a_re_ref, a_im_ref, dt_ref, b_re_ref, b_im_ref, *outs):
        vals = _s5_disc(a_re_ref[...], a_im_ref[...], dt_ref[...], b_re_ref[...], b_im_ref[...])
        for o_ref, val in zip(outs, vals):
            o_ref[...] = val

    sd = jax.ShapeDtypeStruct
    return pl.pallas_call(
        body, name=name,
        out_shape=[sd(a_re.shape, F32), sd(a_re.shape, F32), sd(b_re.shape, F32), sd(b_re.shape, F32)],
    )(a_re, a_im, log_dt, b_re, b_im)


def s5_disc_bwd(a_re, a_im, log_dt, b_re, b_im, d_abar_re, d_abar_im, d_bb_re, d_bb_im, *, name):
    def body(a_re_ref, a_im_ref, dt_ref, b_re_ref, b_im_ref, c0, c1, c2, c3, *outs):
        _, vjp = jax.vjp(_s5_disc, a_re_ref[...], a_im_ref[...], dt_ref[...], b_re_ref[...], b_im_ref[...])
        grads = vjp((c0[...], c1[...], c2[...], c3[...]))
        for o_ref, val in zip(outs, grads):
            o_ref[...] = val

    sd = jax.ShapeDtypeStruct
    return pl.pallas_call(
        body, name=name,
        out_shape=[sd(a_re.shape, F32), sd(a_re.shape, F32), sd(log_dt.shape, F32), sd(b_re.shape, F32), sd(b_re.shape, F32)],
    )(a_re, a_im, log_dt, b_re, b_im, d_abar_re, d_abar_im, d_bb_re, d_bb_im)


def s5_scan(proj, bb_bd, c_bd, abar_re, abar_im, d_skip, *, tl, name):
    L = proj.shape[0]
    W, S2 = bb_bd.shape
    S = S2 // 2

    def body(u_ref, bb_ref, c_ref, ar_ref, ai_ref, d_ref, y_ref, xs_ref, sr_scr, si_scr):
        @pl.when(pl.program_id(0) == 0)
        def _():
            sr_scr[...] = jnp.zeros_like(sr_scr)
            si_scr[...] = jnp.zeros_like(si_scr)

        u = u_ref[...]
        xs_ref[...] = _dot(u, bb_ref[...])
        ar = ar_ref[...]
        ai = ai_ref[...]

        def step(t, carry):
            sr, si = carry
            row = pl.ds(t, 1)
            nsr = ar * sr - ai * si + xs_ref[row, pl.ds(0, S)]
            nsi = ar * si + ai * sr + xs_ref[row, pl.ds(S, S)]
            xs_ref[row, pl.ds(0, S)] = nsr
            xs_ref[row, pl.ds(S, S)] = nsi
            return nsr, nsi

        sr, si = lax.fori_loop(0, tl, step, (sr_scr[...], si_scr[...]), unroll=8)
        sr_scr[...] = sr
        si_scr[...] = si
        y_ref[...] = _dot(xs_ref[...], c_ref[...]) + d_ref[...] * u

    return pl.pallas_call(
        body, name=name, grid=(L // tl,),
        in_specs=[pl.BlockSpec((tl, W), lambda i: (i, 0)), WHOLE_VMEM, WHOLE_VMEM, WHOLE_VMEM, WHOLE_VMEM, WHOLE_VMEM],
        out_specs=[pl.BlockSpec((tl, W), lambda i: (i, 0)), pl.BlockSpec((tl, S2), lambda i: (i, 0))],
        out_shape=[jax.ShapeDtypeStruct((L, W), F32), jax.ShapeDtypeStruct((L, S2), F32)],
        scratch_shapes=[pltpu.VMEM((1, S), F32), pltpu.VMEM((1, S), F32)],
        compiler_params=_params(1),
    )(proj, bb_bd, c_bd, abar_re, abar_im, d_skip)


def s5_scan_bwd(dy, proj, xs, bb_bd, c_bd, abar_re, abar_im, d_skip, *, tl, name):
    L = proj.shape[0]
    W, S2 = bb_bd.shape
    S = S2 // 2
    nt = L // tl
    HALO = 8

    def body(dy_ref, u_ref, xs_ref, xprev_ref, bb_ref, c_ref, ar_ref, ai_ref, d_ref,
             du_ref, dd_ref, dar_ref, dai_ref, dbb_hbm, dc_hbm, lam_scr, lr_scr, li_scr, dbb_scr, dc_scr, sem):
        i = pl.program_id(0)

        @pl.when(i == 0)
        def _():
            for ref in (lr_scr, li_scr, dbb_scr, dc_scr, dd_ref, dar_ref, dai_ref):
                ref[...] = jnp.zeros_like(ref)

        dy = dy_ref[...]
        u = u_ref[...]
        xs_v = xs_ref[...]
        dc_scr[...] += _dot_tn(xs_v, dy)
        lam_scr[...] = _dot_nt(dy, c_ref[...])
        ar = ar_ref[...]
        ai = ai_ref[...]

        def step(s, carry):
            lr, li = carry
            row = pl.ds(tl - 1 - s, 1)
            nlr = lam_scr[row, pl.ds(0, S)] + ar * lr + ai * li
            nli = lam_scr[row, pl.ds(S, S)] - ai * lr + ar * li
            lam_scr[row, pl.ds(0, S)] = nlr
            lam_scr[row, pl.ds(S, S)] = nli
            return nlr, nli

        lr, li = lax.fori_loop(0, tl, step, (lr_scr[...], li_scr[...]), unroll=8)
        lr_scr[...] = lr
        li_scr[...] = li
        lam = lam_scr[...]
        before_tile = jnp.where(i == nt - 1, 0.0, xprev_ref[pl.ds(HALO - 1, 1), :])
        first_row = lax.broadcasted_iota(jnp.int32, (tl, 1), 0) == 0
        x_prev = jnp.where(first_row, before_tile, pltpu.roll(xs_v, 1, 0))
        lam_r, lam_i = lam[:, :S], lam[:, S:]
        xp_r, xp_i = x_prev[:, :S], x_prev[:, S:]
        dar_ref[...] += jnp.sum(lam_r * xp_r + lam_i * xp_i, axis=0, keepdims=True)
        dai_ref[...] += jnp.sum(lam_i * xp_r - lam_r * xp_i, axis=0, keepdims=True)
        du_ref[...] = dy * d_ref[...] + _dot_nt(lam, bb_ref[...])
        dbb_scr[...] += _dot_tn(u, lam)
        dd_ref[...] += jnp.sum(dy * u, axis=0, keepdims=True)

        @pl.when(i == nt - 1)
        def _():
            for src, dst in ((dbb_scr, dbb_hbm), (dc_scr, dc_hbm)):
                cp = pltpu.make_async_copy(src, dst, sem)
                cp.start()
                cp.wait()

    rev = lambda i: (nt - 1 - i, 0)
    const = lambda i: (0, 0)
    halo = lambda i: (jnp.maximum((nt - 1 - i) * (tl // HALO) - 1, 0), 0)
    sd = jax.ShapeDtypeStruct
    return pl.pallas_call(
        body, name=name, grid=(nt,),
        in_specs=[pl.BlockSpec((tl, W), rev), pl.BlockSpec((tl, W), rev), pl.BlockSpec((tl, S2), rev),
                  pl.BlockSpec((HALO, S2), halo), WHOLE_VMEM, WHOLE_VMEM, WHOLE_VMEM, WHOLE_VMEM, WHOLE_VMEM],
        out_specs=[pl.BlockSpec((tl, W), rev), pl.BlockSpec((1, W), const), pl.BlockSpec((1, S), const),
                   pl.BlockSpec((1, S), const), ANY, ANY],
        out_shape=[sd((L, W), F32), sd((1, W), F32), sd((1, S), F32), sd((1, S), F32), sd((W, S2), F32), sd((S2, W), F32)],
        scratch_shapes=[pltpu.VMEM((tl, S2), F32), pltpu.VMEM((1, S), F32), pltpu.VMEM((1, S), F32),
                        pltpu.VMEM((W, S2), F32), pltpu.VMEM((S2, W), F32), pltpu.SemaphoreType.DMA],
        compiler_params=_params(1),
    )(dy, proj, xs, xs, bb_bd, c_bd, abar_re, abar_im, d_skip)


def _glu_gate(gl, z):
    half = gl.shape[1] // 2
    return gl[:, :half] * jax.nn.sigmoid(gl[:, half:]) * jax.nn.silu(z)


def glu_gate(y, proj, w, b, *, tl, name):
    L, W = y.shape

    def body(y_ref, z_ref, w_ref, b_ref, o_ref):
        gl = _dot(jax.nn.gelu(y_ref[...]), w_ref[...]) + b_ref[...]
        o_ref[...] = _glu_gate(gl, z_ref[...])

    return pl.pallas_call(
        body, name=name, grid=(L // tl,),
        in_specs=[pl.BlockSpec((tl, W), lambda i: (i, 0)), pl.BlockSpec((tl, W), lambda i: (i, 1)), WHOLE_VMEM, WHOLE_VMEM],
        out_specs=pl.BlockSpec((tl, W), lambda i: (i, 0)),
        out_shape=jax.ShapeDtypeStruct((L, W), F32),
        compiler_params=_params(1),
    )(y, proj, w, b)


def glu_gate_bwd(dys, y, proj, w, b, *, tl, name):
    L, W = y.shape

    def body(dys_ref, y_ref, z_ref, w_ref, b_ref, dy_ref, dz_ref, dw_ref, db_ref):
        @pl.when(pl.program_id(0) == 0)
        def _():
            dw_ref[...] = jnp.zeros_like(dw_ref)
            db_ref[...] = jnp.zeros_like(db_ref)

        yg, gelu_vjp = jax.vjp(jax.nn.gelu, y_ref[...])
        gl = _dot(yg, w_ref[...]) + b_ref[...]
        _, gate_vjp = jax.vjp(_glu_gate, gl, z_ref[...])
        dgl, dz = gate_vjp(dys_ref[...])
        dz_ref[...] = dz
        dy_ref[...] = gelu_vjp(_dot_nt(dgl, w_ref[...]))[0]
        dw_ref[...] += _dot_tn(yg, dgl)
        db_ref[...] += jnp.sum(dgl, axis=0, keepdims=True)

    row = lambda i: (i, 0)
    const = lambda i: (0, 0)
    sd = jax.ShapeDtypeStruct
    return pl.pallas_call(
        body, name=name, grid=(L // tl,),
        in_specs=[pl.BlockSpec((tl, W), row), pl.BlockSpec((tl, W), row), pl.BlockSpec((tl, W), lambda i: (i, 1)),
                  WHOLE_VMEM, WHOLE_VMEM],
        out_specs=[pl.BlockSpec((tl, W), row), pl.BlockSpec((tl, W), row), pl.BlockSpec((W, 2 * W), const),
                   pl.BlockSpec((1, 2 * W), const)],
        out_shape=[sd((L, W), F32), sd((L, W), F32), sd((W, 2 * W), F32), sd((1, 2 * W), F32)],
        compiler_params=_params(1),
    )(dys, y, proj, w, b)


def _sb_tile(qt, kj, scale, causal, R, suffix_tri):
    z = _dot_nt(qt, kj) * scale
    log_beta = jnp.minimum(z, 0.0) - jnp.log1p(jnp.exp(-jnp.abs(z)))
    log_1m = log_beta - z
    if causal is not None:
        log_1m = jnp.where(causal, log_1m, 0.0)
    w = jnp.exp(log_beta + _dot_split(log_1m, suffix_tri) + R)
    if causal is not None:
        w = jnp.where(causal, w, 0.0)
    return log_beta, log_1m, w


def _tri_masks(t):
    row = lax.broadcasted_iota(jnp.int32, (t, t), 0)
    col = lax.broadcasted_iota(jnp.int32, (t, t), 1)
    return row, col


def attn_fwd(q, k, v, *, tq, name):
    nh, L, hd = q.shape
    scale = hd ** -0.5

    def body(q_ref, k_ref, v_ref, o_ref):
        i = pl.program_id(1)
        qt = q_ref[...]
        row, col = _tri_masks(tq)
        suffix_tri = (row > col).astype(MXU_DTYPE)
        causal = col < row

        def step(j, carry, mask):
            acc, R = carry
            keys = pl.ds(pl.multiple_of(j * tq, tq), tq)
            _, log_1m, w = _sb_tile(qt, k_ref[keys, :], scale, mask, R, suffix_tri)
            return acc + _dot(w, v_ref[keys, :]), R + jnp.sum(log_1m, axis=1, keepdims=True)

        carry = step(i, (jnp.zeros((tq, hd), F32), jnp.zeros((tq, 1), F32)), causal)
        acc, _ = lax.fori_loop(0, i, lambda s, c: step(i - 1 - s, c, None), carry)
        o_ref[...] = acc

    tile = pl.BlockSpec((None, tq, hd), lambda h, i: (h, i, 0))
    whole = pl.BlockSpec((None, L, hd), lambda h, i: (h, 0, 0))
    return pl.pallas_call(
        body, name=name, grid=(nh, L // tq),
        in_specs=[tile, whole, whole], out_specs=tile,
        out_shape=jax.ShapeDtypeStruct((nh, L, hd), F32),
        compiler_params=_params(2),
    )(q, k, v)


def attn_bwd(q, k, v, do, *, tq, name):
    nh, L, hd = q.shape
    nq = L // tq
    scale = hd ** -0.5

    def body(q_ref, k_ref, v_ref, do_ref, dq_ref, dk_ref, dv_ref, e_scr, beta_scr):
        i = pl.program_id(1)

        @pl.when(i == 0)
        def _():
            dk_ref[...] = jnp.zeros_like(dk_ref)
            dv_ref[...] = jnp.zeros_like(dv_ref)

        qt = q_ref[...]
        dot = do_ref[...]
        row, col = _tri_masks(tq)
        suffix_tri = (row > col).astype(MXU_DTYPE)
        prefix_tri = (row < col).astype(MXU_DTYPE)
        causal = col < row

        def rebuild(j, R, mask):
            keys = pl.ds(pl.multiple_of(j * tq, tq), tq)
            log_beta, log_1m, w = _sb_tile(qt, k_ref[keys, :], scale, mask, R, suffix_tri)
            dv_ref[keys, :] += _dot_tn(w, dot)
            e_scr[j] = _dot_nt(dot, v_ref[keys, :]) * w
            beta = jnp.exp(log_beta)
            beta_scr[j] = beta if mask is None else jnp.where(mask, beta, 0.0)
            return R + jnp.sum(log_1m, axis=1, keepdims=True)

        R = rebuild(i, jnp.zeros((tq, 1), F32), causal)
        lax.fori_loop(0, i, lambda s, R: rebuild(i - 1 - s, R, None), R)

        def push(j, carry):
            dq, P = carry
            keys = pl.ds(pl.multiple_of(j * tq, tq), tq)
            e = e_scr[j]
            dz = (e - beta_scr[j] * (e + _dot_split(e, prefix_tri) + P)) * scale
            dk_ref[keys, :] += _dot_tn(dz, qt)
            return dq + _dot(dz, k_ref[keys, :]), P + jnp.sum(e, axis=1, keepdims=True)

        dq, _ = lax.fori_loop(0, i + 1, push, (jnp.zeros((tq, hd), F32), jnp.zeros((tq, 1), F32)))
        dq_ref[...] = dq

    tile = pl.BlockSpec((None, tq, hd), lambda h, i: (h, i, 0))
    whole = pl.BlockSpec((None, L, hd), lambda h, i: (h, 0, 0))
    sd = jax.ShapeDtypeStruct((nh, L, hd), F32)
    return pl.pallas_call(
        body, name=name, grid=(nh, nq),
        in_specs=[tile, whole, whole, tile], out_specs=[tile, whole, whole],
        out_shape=[sd, sd, sd],
        scratch_shapes=[pltpu.VMEM((nq, tq, tq), F32), pltpu.VMEM((nq, tq, tq), F32)],
        compiler_params=_params(2),
    )(q, k, v, do)


def _block_diag(blocks):
    G, m, n = blocks.shape
    eye = jnp.eye(G, dtype=blocks.dtype)
    return (blocks[:, :, None, :] * eye[:, None, :, None]).reshape(G * m, G * n)


def _diag_blocks(dense, m, n):
    G = dense.shape[0] // m
    return jnp.einsum("gmgn->gmn", dense.reshape(G, m, G, n))


def s5_prepare(a_re, a_im, log_dt, b_re, b_im, c_re, c_im, d_skip, *, name):
    G, P = a_re.shape
    b_re_t = b_re.transpose(2, 0, 1)
    b_im_t = b_im.transpose(2, 0, 1)
    log_dt = log_dt.reshape(G, 1)
    abar_re, abar_im, bb_re, bb_im = s5_disc(a_re, a_im, log_dt, b_re_t, b_im_t, name=name)
    bb_bd = jnp.concatenate([_block_diag(bb_re.transpose(1, 0, 2)), _block_diag(bb_im.transpose(1, 0, 2))], axis=1)
    c_bd = jnp.concatenate([_block_diag(c_re.transpose(0, 2, 1)), -_block_diag(c_im.transpose(0, 2, 1))], axis=0)
    return dict(a_re=a_re, a_im=a_im, log_dt=log_dt, b_re_t=b_re_t, b_im_t=b_im_t,
                abar_re=abar_re.reshape(1, G * P), abar_im=abar_im.reshape(1, G * P),
                bb_bd=bb_bd.astype(MXU_DTYPE), c_bd=c_bd.astype(MXU_DTYPE), d=d_skip.reshape(1, -1))


def s5_param_grads(prm, dd, dar, dai, dbb_bd, dc_bd, *, name):
    G, P = prm["a_re"].shape
    H = prm["b_re_t"].shape[0]
    S = G * P
    d_bb_re = _diag_blocks(dbb_bd[:, :S], H, P).transpose(1, 0, 2)
    d_bb_im = _diag_blocks(dbb_bd[:, S:], H, P).transpose(1, 0, 2)
    da_re, da_im, dlog_dt, db_re_t, db_im_t = s5_disc_bwd(
        prm["a_re"], prm["a_im"], prm["log_dt"], prm["b_re_t"], prm["b_im_t"],
        dar.reshape(G, P), dai.reshape(G, P), d_bb_re, d_bb_im, name=name)
    return dict(a_re=da_re, a_im=da_im, log_dt=dlog_dt.reshape(G), b_re=db_re_t.transpose(1, 2, 0), b_im=db_im_t.transpose(1, 2, 0),
                c_re=_diag_blocks(dc_bd[:S], P, H).transpose(0, 2, 1), c_im=-_diag_blocks(dc_bd[S:], P, H).transpose(0, 2, 1),
                d=dd.reshape(-1))


def _silu_gate(o, z):
    return o * jax.nn.silu(z)


def _merge(g_ssm, p_ssm, g_attn, p_attn):
    return jax.nn.sigmoid(g_ssm) * p_ssm + jax.nn.sigmoid(g_attn) * p_attn


def _proj_gates(tl, W, D):
    return [pl.BlockSpec((tl, W), lambda i: (i, 5)), pl.BlockSpec((tl, D), lambda i: (i, 3)), pl.BlockSpec((tl, D), lambda i: (i, 4))]


def merge_out(ys, o, proj, x, w_bs, w_ba, w_out, g_post, *, tl, name):
    L, W = ys.shape
    D = x.shape[1]

    def body(ys_ref, o_ref, z_ref, gs_ref, ga_ref, x_ref, wbs_ref, wba_ref, wout_ref, g_ref, xn_ref, out_ref):
        ya = _silu_gate(o_ref[...], z_ref[...])
        merged = _merge(gs_ref[...], _dot(ys_ref[...], wbs_ref[...]), ga_ref[...], _dot(ya, wba_ref[...]))
        out = _dot(merged, wout_ref[...])
        out_ref[...] = out
        xn_ref[...] = x_ref[...] + _rms(out, g_ref[...])

    row = lambda i: (i, 0)
    sd = jax.ShapeDtypeStruct
    return pl.pallas_call(
        body, name=name, grid=(L // tl,),
        in_specs=[pl.BlockSpec((tl, W), row), pl.BlockSpec((tl, W), row), *_proj_gates(tl, W, D), pl.BlockSpec((tl, D), row),
                  WHOLE_VMEM, WHOLE_VMEM, WHOLE_VMEM, WHOLE_VMEM],
        out_specs=[pl.BlockSpec((tl, D), row), pl.BlockSpec((tl, D), row)],
        out_shape=[sd((L, D), F32), sd((L, D), F32)],
        compiler_params=_params(1),
    )(ys, o, proj, proj, proj, x, w_bs, w_ba, w_out, g_post)


def merge_out_bwd(dxn, out, ys, o, proj, w_bs, w_ba, w_out, g_post, *, tl, name):
    L, W = ys.shape
    D = dxn.shape[1]

    def body(dxn_ref, out_ref, ys_ref, o_ref, z_ref, gs_ref, ga_ref, wbs_ref, wba_ref, wout_ref, g_ref,
             dys_ref, do_ref, dtail_ref, dwout_ref, dwbs_ref, dwba_ref, dg_ref):
        @pl.when(pl.program_id(0) == 0)
        def _():
            for ref in (dwout_ref, dwbs_ref, dwba_ref, dg_ref):
                ref[...] = jnp.zeros_like(ref)

        _, rms_vjp = jax.vjp(_rms, out_ref[...], g_ref[...])
        dout, dg = rms_vjp(dxn_ref[...])
        dg_ref[...] += dg
        ys_v = ys_ref[...]
        ya, gate_vjp = jax.vjp(_silu_gate, o_ref[...], z_ref[...])
        merged, merge_vjp = jax.vjp(_merge, gs_ref[...], _dot(ys_v, wbs_ref[...]), ga_ref[...], _dot(ya, wba_ref[...]))
        dwout_ref[...] += _dot_tn(merged, dout)
        dgs, dps, dga, dpa = merge_vjp(_dot_nt(dout, wout_ref[...]))
        dys_ref[...] = _dot_nt(dps, wbs_ref[...])
        dwbs_ref[...] += _dot_tn(ys_v, dps)
        dwba_ref[...] += _dot_tn(ya, dpa)
        do, dz = gate_vjp(_dot_nt(dpa, wba_ref[...]))
        do_ref[...] = do
        dtail_ref[:, pl.ds(0, W)] = dz
        dtail_ref[:, pl.ds(W, D)] = dgs
        dtail_ref[:, pl.ds(W + D, D)] = dga

    row = lambda i: (i, 0)
    const = lambda i: (0, 0)
    sd = jax.ShapeDtypeStruct
    return pl.pallas_call(
        body, name=name, grid=(L // tl,),
        in_specs=[pl.BlockSpec((tl, D), row), pl.BlockSpec((tl, D), row), pl.BlockSpec((tl, W), row), pl.BlockSpec((tl, W), row),
                  *_proj_gates(tl, W, D), WHOLE_VMEM, WHOLE_VMEM, WHOLE_VMEM, WHOLE_VMEM],
        out_specs=[pl.BlockSpec((tl, W), row), pl.BlockSpec((tl, W), row), pl.BlockSpec((tl, W + 2 * D), row),
                   pl.BlockSpec((D, D), const), pl.BlockSpec((W, D), const), pl.BlockSpec((W, D), const), pl.BlockSpec((1, D), const)],
        out_shape=[sd((L, W), F32), sd((L, W), F32), sd((L, W + 2 * D), F32), sd((D, D), F32), sd((W, D), F32), sd((W, D), F32),
                   sd((1, D), F32)],
        compiler_params=_params(1),
    )(dxn, out, ys, o, proj, proj, proj, w_bs, w_ba, w_out, g_post)


def loss_grad(y, target, *, tl, name):
    L, D = y.shape

    def body(y_ref, t_ref, dy_ref, loss_ref):
        @pl.when(pl.program_id(0) == 0)
        def _():
            loss_ref[...] = jnp.zeros_like(loss_ref)

        err = y_ref[...] - t_ref[...]
        dy_ref[...] = err / D
        loss_ref[...] += 0.5 * jnp.sum(jnp.mean(err * err, axis=-1, keepdims=True), axis=0, keepdims=True)

    row = lambda i: (i, 0)
    return pl.pallas_call(
        body, name=name, grid=(L // tl,),
        in_specs=[pl.BlockSpec((tl, D), row), pl.BlockSpec((tl, D), row)],
        out_specs=[pl.BlockSpec((tl, D), row), pl.BlockSpec((1, 1), lambda i: (0, 0))],
        out_shape=[jax.ShapeDtypeStruct((L, D), F32), jax.ShapeDtypeStruct((1, 1), F32)],
        compiler_params=_params(1),
    )(y, target)


def adamw_from_parts(parts, w, m, v, *, tr, name):
    n, R, C = parts.shape

    def body(p_ref, w_ref, m_ref, v_ref, g_ref, d_ref, nm_ref, nv_ref):
        g = p_ref[0]
        for s in range(1, n):
            g = g + p_ref[s]
        nm = ADAM_B1 * m_ref[...] + (1.0 - ADAM_B1) * g
        nv = ADAM_B2 * v_ref[...] + (1.0 - ADAM_B2) * (g * g)
        m_hat = nm / (1.0 - ADAM_B1 ** ADAM_STEP)
        v_hat = nv / (1.0 - ADAM_B2 ** ADAM_STEP)
        g_ref[...] = g
        d_ref[...] = -ADAM_LR * (m_hat / (jnp.sqrt(v_hat) + ADAM_EPS) + ADAM_WD * w_ref[...])
        nm_ref[...] = nm
        nv_ref[...] = nv

    tile = pl.BlockSpec((tr, C), lambda i: (i, 0))
    sd = jax.ShapeDtypeStruct((R, C), F32)
    return pl.pallas_call(
        body, name=name, grid=(R // tr,),
        in_specs=[pl.BlockSpec((n, tr, C), lambda i: (0, i, 0)), tile, tile, tile],
        out_specs=[tile, tile, tile, tile], out_shape=[sd, sd, sd, sd],
        compiler_params=_params(1),
    )(parts, w, m, v)


PER_PEER = N_DEV - 1


def _position():
    return lax.axis_index("x"), lax.axis_index("y"), lax.axis_index("c")


def _flat(x, y, c):
    return 4 * x + 2 * y + c


def all_gather(shards, *, name):
    n = len(shards)

    def body(*refs):
        ins, outs = refs[:n], refs[n:2 * n]
        send_sems, recv_sems, local_sems = refs[2 * n:]
        x, y, c = _position()
        me, sibling = (x, y, c), (x, y, 1 - c)
        chips = [(1 - x, y), (x, 1 - y), (1 - x, 1 - y)]

        def slot(a, dev):
            return outs[a].at[_flat(*dev)]

        def copy(a, k, block, to, src=None):
            return pltpu.make_async_remote_copy(
                src_ref=slot(a, block) if src is None else src, dst_ref=slot(a, block),
                send_sem=send_sems.at[a * PER_PEER + k], recv_sem=recv_sems.at[a * PER_PEER + k],
                device_id=to, device_id_type=MESH)

        mine = [pltpu.make_async_copy(ins[a], slot(a, me), local_sems.at[a]) for a in range(n)]
        for cp in mine:
            cp.start()
        first = []
        for a in range(n):
            first.append(copy(a, 0, me, sibling, src=ins[a]))
            first += [copy(a, 1 + j, me, (*chip, c), src=ins[a]) for j, chip in enumerate(chips)]
        for cp in first:
            cp.start()
        passed = []
        for j, chip in enumerate(chips):
            for a in range(n):
                copy(a, 1 + j, (*chip, c), me).wait_recv()
                passed.append(copy(a, 4 + j, (*chip, c), sibling))
                passed[-1].start()
        for a in range(n):
            copy(a, 0, sibling, me).wait_recv()
            for j, chip in enumerate(chips):
                copy(a, 4 + j, (*chip, 1 - c), me).wait_recv()
        for cp in first + passed:
            cp.wait_send()
        for cp in mine:
            cp.wait()

    return pl.pallas_call(
        body, name=name,
        in_specs=[ANY] * n, out_specs=[ANY] * n,
        out_shape=[jax.ShapeDtypeStruct((N_DEV, *s.shape), s.dtype) for s in shards],
        scratch_shapes=[pltpu.SemaphoreType.DMA((n * PER_PEER,)), pltpu.SemaphoreType.DMA((n * PER_PEER,)),
                        pltpu.SemaphoreType.DMA((n,))],
    )(*shards)


def exchange_blocks(parts, *, name):
    n = len(parts)

    def body(*refs):
        ins, outs = refs[:n], refs[n:2 * n]
        send_sems, recv_sems, local_sems = refs[2 * n:]
        x, y, c = _position()
        me = _flat(x, y, c)
        peers = []
        for k in range(1, N_DEV):
            peer = (1 - x if k & 4 else x, 1 - y if k & 2 else y, 1 - c if k & 1 else c)
            peers.append((k - 1, peer, _flat(*peer)))

        mine = [pltpu.make_async_copy(ins[a].at[me], outs[a].at[me], local_sems.at[a]) for a in range(n)]
        for cp in mine:
            cp.start()
        sends = [pltpu.make_async_remote_copy(
            src_ref=ins[a].at[peer_flat], dst_ref=outs[a].at[me],
            send_sem=send_sems.at[a * PER_PEER + k], recv_sem=recv_sems.at[a * PER_PEER + k],
            device_id=peer, device_id_type=MESH) for a in range(n) for k, peer, peer_flat in peers]
        for cp in sends:
            cp.start()
        for a in range(n):
            for k, peer, peer_flat in peers:
                pltpu.make_async_remote_copy(
                    src_ref=ins[a].at[me], dst_ref=outs[a].at[peer_flat],
                    send_sem=send_sems.at[a * PER_PEER + k], recv_sem=recv_sems.at[a * PER_PEER + k],
                    device_id=peer, device_id_type=MESH).wait_recv()
        for cp in sends:
            cp.wait_send()
        for cp in mine:
            cp.wait()

    return pl.pallas_call(
        body, name=name,
        in_specs=[ANY] * n, out_specs=[ANY] * n,
        out_shape=[jax.ShapeDtypeStruct(p.shape, p.dtype) for p in parts],
        scratch_shapes=[pltpu.SemaphoreType.DMA((n * PER_PEER,)), pltpu.SemaphoreType.DMA((n * PER_PEER,)),
                        pltpu.SemaphoreType.DMA((n,))],
    )(*parts)


LANES = 128
SMALL = ("pre_norm_g", "post_norm_g", "ssm_a_re", "ssm_a_im", "ssm_log_dt", "ssm_b_re", "ssm_b_im", "ssm_c_re", "ssm_c_im",
         "ssm_d", "b_glu")
COLUMN_SHARDED = ("w_in", "w_glu", "w_branch_ssm", "w_branch_attn")
WEIGHTS = ("pre_norm_g", "post_norm_g", "w_in", "ssm_a_re", "ssm_a_im", "ssm_log_dt", "ssm_b_re", "ssm_b_im", "ssm_c_re",
           "ssm_c_im", "ssm_d", "w_glu", "b_glu", "w_branch_ssm", "w_branch_attn", "w_out")
PACK_ROWS = 256


def _pack(arrays):
    flat = jnp.concatenate([a.reshape(-1) for a in arrays])
    tile = PACK_ROWS * LANES
    return jnp.pad(flat, (0, -flat.size % tile)).reshape(-1, LANES)


def _unpack(packed, like):
    flat, out, at = packed.reshape(-1), [], 0
    for a in like:
        out.append(flat[at:at + a.size].reshape(a.shape))
        at += a.size
    return out


def _row_tile(rows):
    return next(t for t in (256, 128, 64, 32, 16, 8) if rows % t == 0)


def kernel(x, pre_norm_g, post_norm_g, w_in, ssm_a_re, ssm_a_im, ssm_log_dt, ssm_b_re, ssm_b_im, ssm_c_re, ssm_c_im, ssm_d, w_glu, b_glu, w_branch_ssm, w_branch_attn, w_out, loss_target, m_pre_norm_g, m_post_norm_g, m_w_in, m_ssm_a_re, m_ssm_a_im, m_ssm_log_dt, m_ssm_b_re, m_ssm_b_im, m_ssm_c_re, m_ssm_c_im, m_ssm_d, m_w_glu, m_b_glu, m_w_branch_ssm, m_w_branch_attn, m_w_out, v_pre_norm_g, v_post_norm_g, v_w_in, v_ssm_a_re, v_ssm_a_im, v_ssm_log_dt, v_ssm_b_re, v_ssm_b_im, v_ssm_c_re, v_ssm_c_im, v_ssm_d, v_w_glu, v_b_glu, v_w_branch_ssm, v_w_branch_attn, v_w_out):
    given = dict(locals())
    weights = {n: given[n] for n in WEIGHTS}
    depth, D = pre_norm_g.shape
    h = x[0]
    L = h.shape[0]
    W = w_glu.shape[1]
    heads = (L, N_HEADS, HEAD_DIM)

    gathered = all_gather([weights[n].astype(MXU_DTYPE) for n in (*COLUMN_SHARDED, "w_out")], name="gather_weights")
    full = {n: g.transpose(1, 2, 0, 3).reshape(depth, g.shape[2], -1) for n, g in zip(COLUMN_SHARDED, gathered)}
    full["w_out"] = gathered[-1].transpose(1, 0, 2, 3).reshape(depth, -1, D)

    def to_heads(a):
        return a.reshape(heads).transpose(1, 0, 2).astype(MXU_DTYPE)

    def from_heads(a):
        return a.transpose(1, 0, 2).reshape(L, W)

    saved = []
    for l in range(depth):
        prm = s5_prepare(ssm_a_re[l], ssm_a_im[l], ssm_log_dt[l], ssm_b_re[l], ssm_b_im[l], ssm_c_re[l], ssm_c_im[l], ssm_d[l],
                         name=f"s5_disc_{l}")
        proj = norm_proj(h, pre_norm_g[l][None], full["w_in"][l], tl=512, tn=1280, name=f"norm_proj_{l}")
        y, states = s5_scan(proj, prm["bb_bd"], prm["c_bd"], prm["abar_re"], prm["abar_im"], prm["d"], tl=256, name=f"s5_scan_{l}")
        ys = glu_gate(y, proj, full["w_glu"][l], b_glu[l][None], tl=512, name=f"glu_gate_{l}")
        q, k, v = (to_heads(proj[:, b * W:(b + 1) * W]) for b in (2, 3, 4))
        o = from_heads(attn_fwd(q, k, v, tq=256, name=f"attn_fwd_{l}"))
        h_next, out = merge_out(ys, o, proj, h, full["w_branch_ssm"][l], full["w_branch_attn"][l], full["w_out"][l],
                                post_norm_g[l][None], tl=256, name=f"merge_out_{l}")
        saved.append(dict(x=h, prm=prm, proj=proj, y=y, states=states, ys=ys, q=q, k=k, v=v, o=o, out=out))
        h = h_next

    dx, loss_part = loss_grad(h, loss_target[0], tl=512, name="loss_grad")
    loss = lax.psum(loss_part[0, 0], MESH_AXES)

    grads = {n: [None] * depth for n in WEIGHTS}
    for l in reversed(range(depth)):
        s = saved[l]
        prm = s["prm"]
        dys, do, dtail, grads["w_out"][l], grads["w_branch_ssm"][l], grads["w_branch_attn"][l], dg_post = merge_out_bwd(
            dx, s["out"], s["ys"], s["o"], s["proj"], full["w_branch_ssm"][l], full["w_branch_attn"][l], full["w_out"][l],
            post_norm_g[l][None], tl=256, name=f"merge_out_bwd_{l}")
        dq, dk, dv = attn_bwd(s["q"], s["k"], s["v"], to_heads(do), tq=256, name=f"attn_bwd_{l}")
        dy, dz_ssm, grads["w_glu"][l], db_glu = glu_gate_bwd(dys, s["y"], s["proj"], full["w_glu"][l], b_glu[l][None], tl=512,
                                                             name=f"glu_gate_bwd_{l}")
        du, *s5_outs = s5_scan_bwd(dy, s["proj"], s["states"], prm["bb_bd"], prm["c_bd"], prm["abar_re"], prm["abar_im"], prm["d"],
                                   tl=256, name=f"s5_scan_bwd_{l}")
        for n, g in s5_param_grads(prm, *s5_outs, name=f"s5_disc_bwd_{l}").items():
            grads["ssm_" + n][l] = g
        dproj = jnp.concatenate([du, dz_ssm, from_heads(dq), from_heads(dk), from_heads(dv), dtail], axis=1)
        dx, dg_pre = norm_proj_bwd_x(dproj, s["x"], pre_norm_g[l][None], full["w_in"][l], dx, tl=256, name=f"norm_proj_bwd_x_{l}")
        grads["w_in"][l] = norm_proj_bwd_w(dproj, s["x"], pre_norm_g[l][None], tl=512, tn=w_in.shape[2], name=f"norm_proj_bwd_w_{l}")
        grads["pre_norm_g"][l], grads["post_norm_g"][l], grads["b_glu"][l] = dg_pre[0], dg_post[0], db_glu[0]

    def column_blocks(g):
        return g.reshape(g.shape[0], N_DEV, -1).transpose(1, 0, 2)

    parts = [jnp.stack(grads["w_in"], axis=1)]
    parts += [jnp.stack([column_blocks(g) for g in grads[n]], axis=1) for n in COLUMN_SHARDED[1:]]
    parts.append(jnp.stack([g.reshape(N_DEV, -1, D) for g in grads["w_out"]], axis=1))
    received = exchange_blocks(parts, name="exchange_weight_grads")
    results = {}
    for n, r in zip((*COLUMN_SHARDED, "w_out"), received):
        shard = weights[n]
        C = shard.shape[-1]
        as_rows = lambda a: a.reshape(-1, C)
        rows = as_rows(shard).shape[0]
        outs = adamw_from_parts(r.reshape(N_DEV, rows, C), as_rows(shard), as_rows(given["m_" + n]), as_rows(given["v_" + n]),
                                tr=_row_tile(rows), name=f"adamw_{n}")
        results[n] = [o.reshape(shard.shape) for o in outs]

    small = [weights[n] for n in SMALL]
    packed = _pack([jnp.stack(grads[n]) for n in SMALL])
    (everyones,) = all_gather([packed], name="gather_small_grads")
    outs = adamw_from_parts(everyones, _pack(small), _pack([given["m_" + n] for n in SMALL]), _pack([given["v_" + n] for n in SMALL]),
                            tr=PACK_ROWS, name="adamw_small")
    for n, *four in zip(SMALL, *(_unpack(o, small) for o in outs)):
        results[n] = four

    return (loss, dx[None], *(results[n][0] for n in WEIGHTS), *(results[n][1] for n in WEIGHTS),
            *(results[n][2] for n in WEIGHTS), *(results[n][3] for n in WEIGHTS))
```

```python
import jax
import jax.numpy as jnp
from jax import lax
from jax.experimental import pallas as pl
from jax.experimental.pallas import tpu as pltpu

F32 = jnp.float32
MXU_DTYPE = jnp.bfloat16
WIRE_DTYPE = jnp.bfloat16
ATTN_TILE = 256
ATTN_SPAN = 4

EPS = 1e-6
N_HEADS = 8
HEAD_DIM = 64
ADAM_LR = 0.001
ADAM_B1 = 0.9
ADAM_B2 = 0.999
ADAM_EPS = 1e-08
ADAM_WD = 0.01
ADAM_STEP = 10

N_DEV = 8
MESH_AXES = ("x", "y", "c")
VMEM_LIMIT_BYTES = 56 * 1024 * 1024
MESH = pl.DeviceIdType.MESH
ANY = pl.BlockSpec(memory_space=pl.ANY)
WHOLE_VMEM = pl.BlockSpec(memory_space=pltpu.VMEM)


def _params(n_grid_axes):
    return pltpu.CompilerParams(dimension_semantics=("arbitrary",) * n_grid_axes, vmem_limit_bytes=VMEM_LIMIT_BYTES)


def _dot(a, b):
    return jnp.dot(a.astype(MXU_DTYPE), b.astype(MXU_DTYPE), preferred_element_type=F32)


def _dot_nt(a, b):
    return lax.dot_general(a.astype(MXU_DTYPE), b.astype(MXU_DTYPE), (((1,), (1,)), ((), ())), preferred_element_type=F32)


def _dot_tn(a, b):
    return lax.dot_general(a.astype(MXU_DTYPE), b.astype(MXU_DTYPE), (((0,), (0,)), ((), ())), preferred_element_type=F32)


def _rms(v, g):
    return v * lax.rsqrt(jnp.mean(v * v, axis=-1, keepdims=True) + EPS) * g


def norm_proj(x, g, w, *, tl, tn, name):
    L, D = x.shape
    N = w.shape[1]

    def body(x_ref, g_ref, w_ref, o_ref, h_scr):
        @pl.when(pl.program_id(1) == 0)
        def _():
            h_scr[...] = _rms(x_ref[...], g_ref[...]).astype(MXU_DTYPE)

        o_ref[...] = jnp.dot(h_scr[...], w_ref[...], preferred_element_type=F32)

    return pl.pallas_call(
        body, name=name, grid=(L // tl, N // tn),
        in_specs=[pl.BlockSpec((tl, D), lambda i, j: (i, 0)), pl.BlockSpec((1, D), lambda i, j: (0, 0)),
                  pl.BlockSpec((D, tn), lambda i, j: (0, j))],
        out_specs=pl.BlockSpec((tl, tn), lambda i, j: (i, j)),
        out_shape=jax.ShapeDtypeStruct((L, N), F32),
        scratch_shapes=[pltpu.VMEM((tl, D), MXU_DTYPE)],
        compiler_params=_params(2),
    )(x, g, w)


def norm_proj_bwd_x(dproj, x, g, w, dxn, *, tl, name):
    L, D = x.shape
    N = w.shape[1]

    def body(dp_ref, x_ref, g_ref, w_ref, dxn_ref, dx_ref, dg_ref):
        @pl.when(pl.program_id(0) == 0)
        def _():
            dg_ref[...] = jnp.zeros_like(dg_ref)

        dh = _dot_nt(dp_ref[...], w_ref[...])
        _, vjp = jax.vjp(_rms, x_ref[...], g_ref[...])
        dx, dg = vjp(dh)
        dx_ref[...] = dxn_ref[...] + dx
        dg_ref[...] += dg

    return pl.pallas_call(
        body, name=name, grid=(L // tl,),
        in_specs=[pl.BlockSpec((tl, N), lambda i: (i, 0)), pl.BlockSpec((tl, D), lambda i: (i, 0)),
                  pl.BlockSpec((1, D), lambda i: (0, 0)), WHOLE_VMEM, pl.BlockSpec((tl, D), lambda i: (i, 0))],
        out_specs=[pl.BlockSpec((tl, D), lambda i: (i, 0)), pl.BlockSpec((1, D), lambda i: (0, 0))],
        out_shape=[jax.ShapeDtypeStruct((L, D), F32), jax.ShapeDtypeStruct((1, D), F32)],
        compiler_params=_params(1),
    )(dproj, x, g, w, dxn)


def norm_proj_bwd_w(dproj, x, g, *, tl, tn, name):
    L, D = x.shape
    N = dproj.shape[1]

    def body(dp_ref, x_ref, g_ref, dw_ref):
        @pl.when(pl.program_id(1) == 0)
        def _():
            dw_ref[...] = jnp.zeros_like(dw_ref)

        dw_ref[...] += _dot_tn(_rms(x_ref[...], g_ref[...]), dp_ref[...])

    return pl.pallas_call(
        body, name=name, grid=(N // tn, L // tl),
        in_specs=[pl.BlockSpec((tl, tn), lambda j, i: (i, j)), pl.BlockSpec((tl, D), lambda j, i: (i, 0)),
                  pl.BlockSpec((1, D), lambda j, i: (0, 0))],
        out_specs=pl.BlockSpec((None, D, tn), lambda j, i: (j, 0, 0)),
        out_shape=jax.ShapeDtypeStruct((N // tn, D, tn), F32),
        compiler_params=_params(2),
    )(dproj, x, g)


def _s5_disc(a_re, a_im, log_dt, b_re, b_im):
    dt = jnp.exp(log_dt)
    mag = jnp.exp(a_re * dt)
    abar_re = mag * jnp.cos(a_im * dt)
    abar_im = mag * jnp.sin(a_im * dt)
    nr = abar_re - 1.0
    ni = abar_im
    den = a_re * a_re + a_im * a_im
    f_re = (nr * a_re + ni * a_im) / den
    f_im = (ni * a_re - nr * a_im) / den
    bb_re = f_re[None] * b_re - f_im[None] * b_im
    bb_im = f_re[None] * b_im + f_im[None] * b_re
    return abar_re, abar_im, bb_re, bb_im


def s5_disc(a_re, a_im, log_dt, b_re, b_im, *, name):
    def body(a_re_ref, a_im_ref, dt_ref, b_re_ref, b_im_ref, *outs):
        vals = _s5_disc(a_re_ref[...], a_im_ref[...], dt_ref[...], b_re_ref[...], b_im_ref[...])
        for o_ref, val in zip(outs, vals):
            o_ref[...] = val

    sd = jax.ShapeDtypeStruct
    return pl.pallas_call(
        body, name=name,
        out_shape=[sd(a_re.shape, F32), sd(a_re.shape, F32), sd(b_re.shape, F32), sd(b_re.shape, F32)],
    )(a_re, a_im, log_dt, b_re, b_im)


def s5_disc_bwd(a_re, a_im, log_dt, b_re, b_im, d_abar_re, d_abar_im, d_bb_re, d_bb_im, *, name):
    def body(a_re_ref, a_im_ref, dt_ref, b_re_ref, b_im_ref, c0, c1, c2, c3, *outs):
        _, vjp = jax.vjp(_s5_disc, a_re_ref[...], a_im_ref[...], dt_ref[...], b_re_ref[...], b_im_ref[...])
        grads = vjp((c0[...], c1[...], c2[...], c3[...]))
        for o_ref, val in zip(outs, grads):
            o_ref[...] = val

    sd = jax.ShapeDtypeStruct
    return pl.pallas_call(
        body, name=name,
        out_shape=[sd(a_re.shape, F32), sd(a_re.shape, F32), sd(log_dt.shape, F32), sd(b_re.shape, F32), sd(b_re.shape, F32)],
    )(a_re, a_im, log_dt, b_re, b_im, d_abar_re, d_abar_im, d_bb_re, d_bb_im)


def s5_scan(proj, bb_bd, c_bd, abar_re, abar_im, d_skip, *, tl, name):
    L = proj.shape[0]
    W, S2 = bb_bd.shape
    S = S2 // 2

    def body(u_ref, bb_ref, c_ref, ar_ref, ai_ref, d_ref, y_ref, xs_ref, sr_scr, si_scr):
        @pl.when(pl.program_id(0) == 0)
        def _():
            sr_scr[...] = jnp.zeros_like(sr_scr)
            si_scr[...] = jnp.zeros_like(si_scr)

        u = u_ref[...]
        xs_ref[...] = _dot(u, bb_ref[...])
        ar = ar_ref[...]
        ai = ai_ref[...]

        def step(t, carry):
            sr, si = carry
            row = pl.ds(t, 1)
            nsr = ar * sr - ai * si + xs_ref[row, pl.ds(0, S)]
            nsi = ar * si + ai * sr + xs_ref[row, pl.ds(S, S)]
            xs_ref[row, pl.ds(0, S)] = nsr
            xs_ref[row, pl.ds(S, S)] = nsi
            return nsr, nsi

        sr, si = lax.fori_loop(0, tl, step, (sr_scr[...], si_scr[...]), unroll=8)
        sr_scr[...] = sr
        si_scr[...] = si
        y_ref[...] = _dot(xs_ref[...], c_ref[...]) + d_ref[...] * u

    return pl.pallas_call(
        body, name=name, grid=(L // tl,),
        in_specs=[pl.BlockSpec((tl, W), lambda i: (i, 0)), WHOLE_VMEM, WHOLE_VMEM, WHOLE_VMEM, WHOLE_VMEM, WHOLE_VMEM],
        out_specs=[pl.BlockSpec((tl, W), lambda i: (i, 0)), pl.BlockSpec((tl, S2), lambda i: (i, 0))],
        out_shape=[jax.ShapeDtypeStruct((L, W), F32), jax.ShapeDtypeStruct((L, S2), F32)],
        scratch_shapes=[pltpu.VMEM((1, S), F32), pltpu.VMEM((1, S), F32)],
        compiler_params=_params(1),
    )(proj, bb_bd, c_bd, abar_re, abar_im, d_skip)


def s5_scan_bwd(dy, proj, xs, bb_bd, c_bd, abar_re, abar_im, d_skip, *, tl, name):
    L = proj.shape[0]
    W, S2 = bb_bd.shape
    S = S2 // 2
    nt = L // tl
    HALO = 8

    def body(dy_ref, u_ref, xs_ref, xprev_ref, bb_ref, c_ref, ar_ref, ai_ref, d_ref,
             du_ref, dd_ref, dar_ref, dai_ref, dbb_hbm, dc_hbm, lam_scr, lr_scr, li_scr, dbb_scr, dc_scr, sem):
        i = pl.program_id(0)

        @pl.when(i == 0)
        def _():
            for ref in (lr_scr, li_scr, dbb_scr, dc_scr, dd_ref, dar_ref, dai_ref):
                ref[...] = jnp.zeros_like(ref)

        dy = dy_ref[...]
        u = u_ref[...]
        xs_v = xs_ref[...]
        dc_scr[...] += _dot_tn(xs_v, dy)
        lam_scr[...] = _dot_nt(dy, c_ref[...])
        ar = ar_ref[...]
        ai = ai_ref[...]

        def step(s, carry):
            lr, li = carry
            row = pl.ds(tl - 1 - s, 1)
            nlr = lam_scr[row, pl.ds(0, S)] + ar * lr + ai * li
            nli = lam_scr[row, pl.ds(S, S)] - ai * lr + ar * li
            lam_scr[row, pl.ds(0, S)] = nlr
            lam_scr[row, pl.ds(S, S)] = nli
            return nlr, nli

        lr, li = lax.fori_loop(0, tl, step, (lr_scr[...], li_scr[...]), unroll=8)
        lr_scr[...] = lr
        li_scr[...] = li
        lam = lam_scr[...]
        before_tile = jnp.where(i == nt - 1, 0.0, xprev_ref[pl.ds(HALO - 1, 1), :])
        first_row = lax.broadcasted_iota(jnp.int32, (tl, 1), 0) == 0
        x_prev = jnp.where(first_row, before_tile, pltpu.roll(xs_v, 1, 0))
        lam_r, lam_i = lam[:, :S], lam[:, S:]
        xp_r, xp_i = x_prev[:, :S], x_prev[:, S:]
        dar_ref[...] += jnp.sum(lam_r * xp_r + lam_i * xp_i, axis=0, keepdims=True)
        dai_ref[...] += jnp.sum(lam_i * xp_r - lam_r * xp_i, axis=0, keepdims=True)
        du_ref[...] = dy * d_ref[...] + _dot_nt(lam, bb_ref[...])
        dbb_scr[...] += _dot_tn(u, lam)
        dd_ref[...] += jnp.sum(dy * u, axis=0, keepdims=True)

        @pl.when(i == nt - 1)
        def _():
            for src, dst in ((dbb_scr, dbb_hbm), (dc_scr, dc_hbm)):
                cp = pltpu.make_async_copy(src, dst, sem)
                cp.start()
                cp.wait()

    rev = lambda i: (nt - 1 - i, 0)
    const = lambda i: (0, 0)
    halo = lambda i: (jnp.maximum((nt - 1 - i) * (tl // HALO) - 1, 0), 0)
    sd = jax.ShapeDtypeStruct
    return pl.pallas_call(
        body, name=name, grid=(nt,),
        in_specs=[pl.BlockSpec((tl, W), rev), pl.BlockSpec((tl, W), rev), pl.BlockSpec((tl, S2), rev),
                  pl.BlockSpec((HALO, S2), halo), WHOLE_VMEM, WHOLE_VMEM, WHOLE_VMEM, WHOLE_VMEM, WHOLE_VMEM],
        out_specs=[pl.BlockSpec((tl, W), rev), pl.BlockSpec((1, W), const), pl.BlockSpec((1, S), const),
                   pl.BlockSpec((1, S), const), ANY, ANY],
        out_shape=[sd((L, W), F32), sd((1, W), F32), sd((1, S), F32), sd((1, S), F32), sd((W, S2), F32), sd((S2, W), F32)],
        scratch_shapes=[pltpu.VMEM((tl, S2), F32), pltpu.VMEM((1, S), F32), pltpu.VMEM((1, S), F32),
                        pltpu.VMEM((W, S2), F32), pltpu.VMEM((S2, W), F32), pltpu.SemaphoreType.DMA],
        compiler_params=_params(1),
    )(dy, proj, xs, xs, bb_bd, c_bd, abar_re, abar_im, d_skip)


def _glu_gate(gl, z):
    half = gl.shape[1] // 2
    return gl[:, :half] * jax.nn.sigmoid(gl[:, half:]) * jax.nn.silu(z)


def glu_gate(y, proj, w, b, *, tl, name):
    L, W = y.shape

    def body(y_ref, z_ref, w_ref, b_ref, o_ref):
        gl = _dot(jax.nn.gelu(y_ref[...]), w_ref[...]) + b_ref[...]
        o_ref[...] = _glu_gate(gl, z_ref[...])

    return pl.pallas_call(
        body, name=name, grid=(L // tl,),
        in_specs=[pl.BlockSpec((tl, W), lambda i: (i, 0)), pl.BlockSpec((tl, W), lambda i: (i, 1)), WHOLE_VMEM, WHOLE_VMEM],
        out_specs=pl.BlockSpec((tl, W), lambda i: (i, 0)),
        out_shape=jax.ShapeDtypeStruct((L, W), F32),
        compiler_params=_params(1),
    )(y, proj, w, b)


def glu_gate_bwd(dys, y, proj, w, b, *, tl, name):
    L, W = y.shape

    def body(dys_ref, y_ref, z_ref, w_ref, b_ref, dy_ref, dz_ref, dw_ref, db_ref):
        @pl.when(pl.program_id(0) == 0)
        def _():
            dw_ref[...] = jnp.zeros_like(dw_ref)
            db_ref[...] = jnp.zeros_like(db_ref)

        yg, gelu_vjp = jax.vjp(jax.nn.gelu, y_ref[...])
        gl = _dot(yg, w_ref[...]) + b_ref[...]
        _, gate_vjp = jax.vjp(_glu_gate, gl, z_ref[...])
        dgl, dz = gate_vjp(dys_ref[...])
        dz_ref[...] = dz
        dy_ref[...] = gelu_vjp(_dot_nt(dgl, w_ref[...]))[0]
        dw_ref[...] += _dot_tn(yg, dgl)
        db_ref[...] += jnp.sum(dgl, axis=0, keepdims=True)

    row = lambda i: (i, 0)
    const = lambda i: (0, 0)
    sd = jax.ShapeDtypeStruct
    return pl.pallas_call(
        body, name=name, grid=(L // tl,),
        in_specs=[pl.BlockSpec((tl, W), row), pl.BlockSpec((tl, W), row), pl.BlockSpec((tl, W), lambda i: (i, 1)),
                  WHOLE_VMEM, WHOLE_VMEM],
        out_specs=[pl.BlockSpec((tl, W), row), pl.BlockSpec((tl, W), row), pl.BlockSpec((W, 2 * W), const),
                   pl.BlockSpec((1, 2 * W), const)],
        out_shape=[sd((L, W), F32), sd((L, W), F32), sd((W, 2 * W), F32), sd((1, 2 * W), F32)],
        compiler_params=_params(1),
    )(dys, y, proj, w, b)


def _logs(z, mask):
    t = jnp.minimum(z, 0.0)
    log_beta = t - jnp.log(1.0 + jnp.exp(t + t - z))
    log_1m = log_beta - z
    if mask is not None:
        log_1m = jnp.where(mask, log_1m, 0.0)
    hi = log_1m.astype(MXU_DTYPE)
    lo = (log_1m - hi.astype(F32)).astype(MXU_DTYPE)
    return log_beta, log_1m, jnp.concatenate([hi, lo], axis=1)


def _weights(q_i, k_ref, tiles, masks, R, tri2, tq):
    keys = [pl.ds(pl.multiple_of(t * tq, tq), tq) for t in tiles]
    z = [_dot_nt(q_i, k_ref[ks, :]) for ks in keys]
    logs = [_logs(zv, m) for zv, m in zip(z, masks)]
    after = [jnp.dot(split, tri2, preferred_element_type=F32) for _, _, split in logs]
    w = []
    for (log_beta, log_1m, _), a, m in zip(logs, after, masks):
        wt = jnp.exp(log_beta + a + R)
        w.append(wt if m is None else jnp.where(m, wt, 0.0))
        R = R + jnp.sum(log_1m, axis=1, keepdims=True)
    return keys, [lg[0] for lg in logs], w, R


def _iotas(t):
    return lax.broadcasted_iota(jnp.int32, (t, t), 0), lax.broadcasted_iota(jnp.int32, (t, t), 1)


def _suffix_tri2(row, col):
    tri = (row > col).astype(MXU_DTYPE)
    return jnp.concatenate([tri, tri], axis=0)


def attn_fwd(q, k, vt, *, tq, nk, name):
    nh, L, hd = q.shape
    assert L % (tq * nk) == 0

    def body(q_ref, k_ref, vt_ref, o_ref):
        i = pl.program_id(1)
        q_i = q_ref[...]
        row, col = _iotas(tq)
        tri2 = _suffix_tri2(row, col)
        J = i // nk

        def step(j, carry, diag):
            acc, R = carry
            tiles = [j * nk + h for h in reversed(range(nk))]
            masks = [(col + t * tq < row + i * tq) if diag else None for t in tiles]
            keys, _, w, R = _weights(q_i, k_ref, tiles, masks, R, tri2, tq)
            for ks, wt in zip(keys, w):
                acc = acc + _dot_nt(vt_ref[:, ks], wt)
            return acc, R

        carry = step(J, (jnp.zeros((hd, tq), F32), jnp.zeros((tq, 1), F32)), True)
        acc, _ = lax.fori_loop(0, J, lambda s, cr: step(J - 1 - s, cr, False), carry)
        o_ref[...] = acc

    return pl.pallas_call(
        body, name=name, grid=(nh, L // tq),
        in_specs=[pl.BlockSpec((None, tq, hd), lambda h, i: (h, i, 0)), pl.BlockSpec((None, L, hd), lambda h, i: (h, 0, 0)),
                  pl.BlockSpec((None, hd, L), lambda h, i: (h, 0, 0))],
        out_specs=pl.BlockSpec((None, hd, tq), lambda h, i: (h, 0, i)),
        out_shape=jax.ShapeDtypeStruct((nh, hd, L), F32),
        compiler_params=_params(2),
    )(q, k, vt)


def attn_bwd(q, k, v, do, qt, kt, dot, *, tq, nk, name):
    nh, L, hd = q.shape
    assert L % (tq * nk) == 0
    nq = L // tq
    scale = hd ** -0.5

    def body(q_ref, k_ref, v_ref, do_ref, qt_ref, kt_ref, dot_ref, dq_ref, dk_ref, dv_ref, e_scr, beta_scr):
        i = pl.program_id(1)

        @pl.when(i == 0)
        def _():
            dk_ref[...] = jnp.zeros_like(dk_ref)
            dv_ref[...] = jnp.zeros_like(dv_ref)

        q_i = q_ref[...]
        do_i = do_ref[...]
        dot_i = dot_ref[...]
        qt_i = qt_ref[...]
        row, col = _iotas(tq)
        tri2 = _suffix_tri2(row, col)
        prefix_tri = (row < col).astype(MXU_DTYPE)
        J = i // nk

        def rebuild(j, R, diag):
            tiles = [j * nk + h for h in reversed(range(nk))]
            masks = [(col + t * tq < row + i * tq) if diag else None for t in tiles]
            dw = [_dot_nt(do_i, v_ref[pl.ds(pl.multiple_of(t * tq, tq), tq), :]) for t in tiles]
            keys, log_beta, w, R = _weights(q_i, k_ref, tiles, masks, R, tri2, tq)
            for t, ks, lb, wt, dwt, m in zip(tiles, keys, log_beta, w, dw, masks):
                dv_ref[:, ks] += _dot(dot_i, wt)
                e_scr[t] = dwt * wt
                beta = jnp.exp(lb)
                beta_scr[t] = (beta if m is None else jnp.where(m, beta, 0.0)).astype(beta_scr.dtype)
            return R

        R = rebuild(J, jnp.zeros((tq, 1), F32), True)
        lax.fori_loop(0, J, lambda s, R: rebuild(J - 1 - s, R, False), R)

        def push(j, carry):
            dq, P = carry
            tiles = [j * nk + h for h in range(nk)]
            keys = [pl.ds(pl.multiple_of(t * tq, tq), tq) for t in tiles]
            e = [e_scr[t] for t in tiles]
            before = [_dot(ev, prefix_tri) for ev in e]
            for t, ks, ev, bv in zip(tiles, keys, e, before):
                dz = (ev - beta_scr[t].astype(F32) * (ev + bv + P)).astype(MXU_DTYPE)
                dk_ref[:, ks] += _dot(qt_i, dz)
                dq = dq + _dot_nt(kt_ref[:, ks], dz)
                P = P + jnp.sum(ev, axis=1, keepdims=True)
            return dq, P

        dq, _ = lax.fori_loop(0, J + 1, push, (jnp.zeros((hd, tq), F32), jnp.zeros((tq, 1), F32)))
        dq_ref[...] = dq * scale

    tile = pl.BlockSpec((None, tq, hd), lambda h, i: (h, i, 0))
    whole = pl.BlockSpec((None, L, hd), lambda h, i: (h, 0, 0))
    tile_t = pl.BlockSpec((None, hd, tq), lambda h, i: (h, 0, i))
    whole_t = pl.BlockSpec((None, hd, L), lambda h, i: (h, 0, 0))
    sd = jax.ShapeDtypeStruct((nh, hd, L), F32)
    return pl.pallas_call(
        body, name=name, grid=(nh, nq),
        in_specs=[tile, whole, whole, tile, tile_t, whole_t, tile_t], out_specs=[tile_t, whole_t, whole_t],
        out_shape=[sd, sd, sd],
        scratch_shapes=[pltpu.VMEM((nq, tq, tq), F32), pltpu.VMEM((nq, tq, tq), MXU_DTYPE)],
        compiler_params=_params(2),
    )(q, k, v, do, qt, kt, dot)


def _block_diag(blocks):
    G, m, n = blocks.shape
    eye = jnp.eye(G, dtype=blocks.dtype)
    return (blocks[:, :, None, :] * eye[:, None, :, None]).reshape(G * m, G * n)


def _diag_blocks(dense, m, n):
    G = dense.shape[0] // m
    return jnp.einsum("gmgn->gmn", dense.reshape(G, m, G, n))


def s5_prepare(a_re, a_im, log_dt, b_re, b_im, c_re, c_im, d_skip, *, name):
    G, P = a_re.shape
    b_re_t = b_re.transpose(2, 0, 1)
    b_im_t = b_im.transpose(2, 0, 1)
    log_dt = log_dt.reshape(G, 1)
    abar_re, abar_im, bb_re, bb_im = s5_disc(a_re, a_im, log_dt, b_re_t, b_im_t, name=name)
    bb_bd = jnp.concatenate([_block_diag(bb_re.transpose(1, 0, 2)), _block_diag(bb_im.transpose(1, 0, 2))], axis=1)
    c_bd = jnp.concatenate([_block_diag(c_re.transpose(0, 2, 1)), -_block_diag(c_im.transpose(0, 2, 1))], axis=0)
    return dict(a_re=a_re, a_im=a_im, log_dt=log_dt, b_re_t=b_re_t, b_im_t=b_im_t,
                abar_re=abar_re.reshape(1, G * P), abar_im=abar_im.reshape(1, G * P),
                bb_bd=bb_bd.astype(MXU_DTYPE), c_bd=c_bd.astype(MXU_DTYPE), d=d_skip.reshape(1, -1))


def s5_param_grads(prm, dd, dar, dai, dbb_bd, dc_bd, *, name):
    G, P = prm["a_re"].shape
    H = prm["b_re_t"].shape[0]
    S = G * P
    d_bb_re = _diag_blocks(dbb_bd[:, :S], H, P).transpose(1, 0, 2)
    d_bb_im = _diag_blocks(dbb_bd[:, S:], H, P).transpose(1, 0, 2)
    da_re, da_im, dlog_dt, db_re_t, db_im_t = s5_disc_bwd(
        prm["a_re"], prm["a_im"], prm["log_dt"], prm["b_re_t"], prm["b_im_t"],
        dar.reshape(G, P), dai.reshape(G, P), d_bb_re, d_bb_im, name=name)
    return dict(a_re=da_re, a_im=da_im, log_dt=dlog_dt.reshape(G), b_re=db_re_t.transpose(1, 2, 0), b_im=db_im_t.transpose(1, 2, 0),
                c_re=_diag_blocks(dc_bd[:S], P, H).transpose(0, 2, 1), c_im=-_diag_blocks(dc_bd[S:], P, H).transpose(0, 2, 1),
                d=dd.reshape(-1))


def _silu_gate(o, z):
    return o * jax.nn.silu(z)


def _merge(g_ssm, p_ssm, g_attn, p_attn):
    return jax.nn.sigmoid(g_ssm) * p_ssm + jax.nn.sigmoid(g_attn) * p_attn


def _proj_gates(tl, W, D):
    return [pl.BlockSpec((tl, W), lambda i: (i, 5)), pl.BlockSpec((tl, D), lambda i: (i, 3)), pl.BlockSpec((tl, D), lambda i: (i, 4))]


def merge_out(ys, o, proj, x, w_bs, w_ba, w_out, g_post, *, tl, name):
    L, W = ys.shape
    D = x.shape[1]

    def body(ys_ref, o_ref, z_ref, gs_ref, ga_ref, x_ref, wbs_ref, wba_ref, wout_ref, g_ref, xn_ref, out_ref):
        ya = _silu_gate(o_ref[...], z_ref[...])
        merged = _merge(gs_ref[...], _dot(ys_ref[...], wbs_ref[...]), ga_ref[...], _dot(ya, wba_ref[...]))
        out = _dot(merged, wout_ref[...])
        out_ref[...] = out
        xn_ref[...] = x_ref[...] + _rms(out, g_ref[...])

    row = lambda i: (i, 0)
    sd = jax.ShapeDtypeStruct
    return pl.pallas_call(
        body, name=name, grid=(L // tl,),
        in_specs=[pl.BlockSpec((tl, W), row), pl.BlockSpec((tl, W), row), *_proj_gates(tl, W, D), pl.BlockSpec((tl, D), row),
                  WHOLE_VMEM, WHOLE_VMEM, WHOLE_VMEM, WHOLE_VMEM],
        out_specs=[pl.BlockSpec((tl, D), row), pl.BlockSpec((tl, D), row)],
        out_shape=[sd((L, D), F32), sd((L, D), F32)],
        compiler_params=_params(1),
    )(ys, o, proj, proj, proj, x, w_bs, w_ba, w_out, g_post)


def merge_out_bwd(dxn, out, ys, o, proj, w_bs, w_ba, w_out, g_post, *, tl, name):
    L, W = ys.shape
    D = dxn.shape[1]

    def body(dxn_ref, out_ref, ys_ref, o_ref, z_ref, gs_ref, ga_ref, wbs_ref, wba_ref, wout_ref, g_ref,
             dys_ref, do_ref, dtail_ref, dwout_ref, dwbs_ref, dwba_ref, dg_ref):
        @pl.when(pl.program_id(0) == 0)
        def _():
            for ref in (dwout_ref, dwbs_ref, dwba_ref, dg_ref):
                ref[...] = jnp.zeros_like(ref)

        _, rms_vjp = jax.vjp(_rms, out_ref[...], g_ref[...])
        dout, dg = rms_vjp(dxn_ref[...])
        dg_ref[...] += dg
        ys_v = ys_ref[...]
        ya, gate_vjp = jax.vjp(_silu_gate, o_ref[...], z_ref[...])
        merged, merge_vjp = jax.vjp(_merge, gs_ref[...], _dot(ys_v, wbs_ref[...]), ga_ref[...], _dot(ya, wba_ref[...]))
        dwout_ref[...] += _dot_tn(merged, dout)
        dgs, dps, dga, dpa = merge_vjp(_dot_nt(dout, wout_ref[...]))
        dys_ref[...] = _dot_nt(dps, wbs_ref[...])
        dwbs_ref[...] += _dot_tn(ys_v, dps)
        dwba_ref[...] += _dot_tn(ya, dpa)
        do, dz = gate_vjp(_dot_nt(dpa, wba_ref[...]))
        do_ref[...] = do
        dtail_ref[:, pl.ds(0, W)] = dz
        dtail_ref[:, pl.ds(W, D)] = dgs
        dtail_ref[:, pl.ds(W + D, D)] = dga

    row = lambda i: (i, 0)
    const = lambda i: (0, 0)
    sd = jax.ShapeDtypeStruct
    return pl.pallas_call(
        body, name=name, grid=(L // tl,),
        in_specs=[pl.BlockSpec((tl, D), row), pl.BlockSpec((tl, D), row), pl.BlockSpec((tl, W), row), pl.BlockSpec((tl, W), row),
                  *_proj_gates(tl, W, D), WHOLE_VMEM, WHOLE_VMEM, WHOLE_VMEM, WHOLE_VMEM],
        out_specs=[pl.BlockSpec((tl, W), row), pl.BlockSpec((tl, W), row), pl.BlockSpec((tl, W + 2 * D), row),
                   pl.BlockSpec((D, D), const), pl.BlockSpec((W, D), const), pl.BlockSpec((W, D), const), pl.BlockSpec((1, D), const)],
        out_shape=[sd((L, W), F32), sd((L, W), F32), sd((L, W + 2 * D), F32), sd((D, D), F32), sd((W, D), F32), sd((W, D), F32),
                   sd((1, D), F32)],
        compiler_params=_params(1),
    )(dxn, out, ys, o, proj, proj, proj, w_bs, w_ba, w_out, g_post)


def loss_grad(y, target, *, tl, name):
    L, D = y.shape

    def body(y_ref, t_ref, dy_ref, loss_ref):
        @pl.when(pl.program_id(0) == 0)
        def _():
            loss_ref[...] = jnp.zeros_like(loss_ref)

        err = y_ref[...] - t_ref[...]
        dy_ref[...] = err / D
        loss_ref[...] += 0.5 * jnp.sum(jnp.mean(err * err, axis=-1, keepdims=True), axis=0, keepdims=True)

    row = lambda i: (i, 0)
    return pl.pallas_call(
        body, name=name, grid=(L // tl,),
        in_specs=[pl.BlockSpec((tl, D), row), pl.BlockSpec((tl, D), row)],
        out_specs=[pl.BlockSpec((tl, D), row), pl.BlockSpec((1, 1), lambda i: (0, 0))],
        out_shape=[jax.ShapeDtypeStruct((L, D), F32), jax.ShapeDtypeStruct((1, 1), F32)],
        compiler_params=_params(1),
    )(y, target)


def adamw_from_parts(parts, w, m, v, *, tr, name):
    n, R, C = parts.shape

    def body(p_ref, w_ref, m_ref, v_ref, g_ref, d_ref, nm_ref, nv_ref):
        g = p_ref[0].astype(F32)
        for s in range(1, n):
            g = g + p_ref[s].astype(F32)
        nm = ADAM_B1 * m_ref[...] + (1.0 - ADAM_B1) * g
        nv = ADAM_B2 * v_ref[...] + (1.0 - ADAM_B2) * (g * g)
        m_hat = nm / (1.0 - ADAM_B1 ** ADAM_STEP)
        v_hat = nv / (1.0 - ADAM_B2 ** ADAM_STEP)
        g_ref[...] = g
        d_ref[...] = -ADAM_LR * (m_hat / (jnp.sqrt(v_hat) + ADAM_EPS) + ADAM_WD * w_ref[...])
        nm_ref[...] = nm
        nv_ref[...] = nv

    tile = pl.BlockSpec((tr, C), lambda i: (i, 0))
    sd = jax.ShapeDtypeStruct((R, C), F32)
    return pl.pallas_call(
        body, name=name, grid=(R // tr,),
        in_specs=[pl.BlockSpec((n, tr, C), lambda i: (0, i, 0)), tile, tile, tile],
        out_specs=[tile, tile, tile, tile], out_shape=[sd, sd, sd, sd],
        compiler_params=_params(1),
    )(parts, w, m, v)


PER_PEER = N_DEV - 1


def _position():
    return lax.axis_index("x"), lax.axis_index("y"), lax.axis_index("c")


def _flat(x, y, c):
    return 4 * x + 2 * y + c


def all_gather(shards, *, name):
    n = len(shards)

    def body(*refs):
        ins, outs = refs[:n], refs[n:2 * n]
        send_sems, recv_sems, local_sems = refs[2 * n:]
        x, y, c = _position()
        me, sibling = (x, y, c), (x, y, 1 - c)
        chips = [(1 - x, y), (x, 1 - y), (1 - x, 1 - y)]

        def slot(a, dev):
            return outs[a].at[_flat(*dev)]

        def copy(a, k, block, to, src=None):
            return pltpu.make_async_remote_copy(
                src_ref=slot(a, block) if src is None else src, dst_ref=slot(a, block),
                send_sem=send_sems.at[a * PER_PEER + k], recv_sem=recv_sems.at[a * PER_PEER + k],
                device_id=to, device_id_type=MESH)

        mine = [pltpu.make_async_copy(ins[a], slot(a, me), local_sems.at[a]) for a in range(n)]
        for cp in mine:
            cp.start()
        first = []
        for a in range(n):
            first.append(copy(a, 0, me, sibling, src=ins[a]))
            first += [copy(a, 1 + j, me, (*chip, c), src=ins[a]) for j, chip in enumerate(chips)]
        for cp in first:
            cp.start()
        passed = []
        for j, chip in enumerate(chips):
            for a in range(n):
                copy(a, 1 + j, (*chip, c), me).wait_recv()
                passed.append(copy(a, 4 + j, (*chip, c), sibling))
                passed[-1].start()
        for a in range(n):
            copy(a, 0, sibling, me).wait_recv()
            for j, chip in enumerate(chips):
                copy(a, 4 + j, (*chip, 1 - c), me).wait_recv()
        for cp in first + passed:
            cp.wait_send()
        for cp in mine:
            cp.wait()

    return pl.pallas_call(
        body, name=name,
        in_specs=[ANY] * n, out_specs=[ANY] * n,
        out_shape=[jax.ShapeDtypeStruct((N_DEV, *s.shape), s.dtype) for s in shards],
        scratch_shapes=[pltpu.SemaphoreType.DMA((n * PER_PEER,)), pltpu.SemaphoreType.DMA((n * PER_PEER,)),
                        pltpu.SemaphoreType.DMA((n,))],
    )(*shards)


def exchange_blocks(parts, *, name):
    n = len(parts)

    def body(*refs):
        ins, outs = refs[:n], refs[n:2 * n]
        send_sems, recv_sems, local_sems = refs[2 * n:]
        x, y, c = _position()
        me = _flat(x, y, c)
        peers = []
        for k in range(1, N_DEV):
            peer = (1 - x if k & 4 else x, 1 - y if k & 2 else y, 1 - c if k & 1 else c)
            peers.append((k - 1, peer, _flat(*peer)))

        mine = [pltpu.make_async_copy(ins[a].at[me], outs[a].at[me], local_sems.at[a]) for a in range(n)]
        for cp in mine:
            cp.start()
        sends = [pltpu.make_async_remote_copy(
            src_ref=ins[a].at[peer_flat], dst_ref=outs[a].at[me],
            send_sem=send_sems.at[a * PER_PEER + k], recv_sem=recv_sems.at[a * PER_PEER + k],
            device_id=peer, device_id_type=MESH) for a in range(n) for k, peer, peer_flat in peers]
        for cp in sends:
            cp.start()
        for a in range(n):
            for k, peer, peer_flat in peers:
                pltpu.make_async_remote_copy(
                    src_ref=ins[a].at[me], dst_ref=outs[a].at[peer_flat],
                    send_sem=send_sems.at[a * PER_PEER + k], recv_sem=recv_sems.at[a * PER_PEER + k],
                    device_id=peer, device_id_type=MESH).wait_recv()
        for cp in sends:
            cp.wait_send()
        for cp in mine:
            cp.wait()

    return pl.pallas_call(
        body, name=name,
        in_specs=[ANY] * n, out_specs=[ANY] * n,
        out_shape=[jax.ShapeDtypeStruct(p.shape, p.dtype) for p in parts],
        scratch_shapes=[pltpu.SemaphoreType.DMA((n * PER_PEER,)), pltpu.SemaphoreType.DMA((n * PER_PEER,)),
                        pltpu.SemaphoreType.DMA((n,))],
    )(*parts)


LANES = 128
SMALL = ("pre_norm_g", "post_norm_g", "ssm_a_re", "ssm_a_im", "ssm_log_dt", "ssm_b_re", "ssm_b_im", "ssm_c_re", "ssm_c_im",
         "ssm_d", "b_glu")
COLUMN_SHARDED = ("w_in", "w_glu", "w_branch_ssm", "w_branch_attn")
WEIGHTS = ("pre_norm_g", "post_norm_g", "w_in", "ssm_a_re", "ssm_a_im", "ssm_log_dt", "ssm_b_re", "ssm_b_im", "ssm_c_re",
           "ssm_c_im", "ssm_d", "w_glu", "b_glu", "w_branch_ssm", "w_branch_attn", "w_out")
PACK_ROWS = 256


def _pack(arrays):
    flat = jnp.concatenate([a.reshape(-1) for a in arrays])
    tile = PACK_ROWS * LANES
    return jnp.pad(flat, (0, -flat.size % tile)).reshape(-1, LANES)


def _unpack(packed, like):
    flat, out, at = packed.reshape(-1), [], 0
    for a in like:
        out.append(flat[at:at + a.size].reshape(a.shape))
        at += a.size
    return out


def _row_tile(rows):
    return next(t for t in (256, 128, 64, 32, 16, 8) if rows % t == 0)


def kernel(x, pre_norm_g, post_norm_g, w_in, ssm_a_re, ssm_a_im, ssm_log_dt, ssm_b_re, ssm_b_im, ssm_c_re, ssm_c_im, ssm_d, w_glu, b_glu, w_branch_ssm, w_branch_attn, w_out, loss_target, m_pre_norm_g, m_post_norm_g, m_w_in, m_ssm_a_re, m_ssm_a_im, m_ssm_log_dt, m_ssm_b_re, m_ssm_b_im, m_ssm_c_re, m_ssm_c_im, m_ssm_d, m_w_glu, m_b_glu, m_w_branch_ssm, m_w_branch_attn, m_w_out, v_pre_norm_g, v_post_norm_g, v_w_in, v_ssm_a_re, v_ssm_a_im, v_ssm_log_dt, v_ssm_b_re, v_ssm_b_im, v_ssm_c_re, v_ssm_c_im, v_ssm_d, v_w_glu, v_b_glu, v_w_branch_ssm, v_w_branch_attn, v_w_out):
    given = dict(locals())
    weights = {n: given[n] for n in WEIGHTS}
    depth, D = pre_norm_g.shape
    h = x[0]
    L = h.shape[0]
    W = w_glu.shape[1]
    heads = (L, N_HEADS, HEAD_DIM)

    gathered = all_gather([weights[n].astype(MXU_DTYPE) for n in (*COLUMN_SHARDED, "w_out")], name="gather_weights")
    full = {n: g.transpose(1, 2, 0, 3).reshape(depth, g.shape[2], -1) for n, g in zip(COLUMN_SHARDED, gathered)}
    full["w_out"] = gathered[-1].transpose(1, 0, 2, 3).reshape(depth, -1, D)

    def to_heads(a):
        a = a.reshape(heads).astype(MXU_DTYPE)
        return a.transpose(1, 0, 2), a.transpose(1, 2, 0)

    def from_heads_t(a):
        return a.transpose(2, 0, 1).reshape(L, W)

    saved = []
    for l in range(depth):
        prm = s5_prepare(ssm_a_re[l], ssm_a_im[l], ssm_log_dt[l], ssm_b_re[l], ssm_b_im[l], ssm_c_re[l], ssm_c_im[l], ssm_d[l],
                         name=f"s5_disc_{l}")
        proj = norm_proj(h, pre_norm_g[l][None], full["w_in"][l], tl=512, tn=1280, name=f"norm_proj_{l}")
        y, states = s5_scan(proj, prm["bb_bd"], prm["c_bd"], prm["abar_re"], prm["abar_im"], prm["d"], tl=256, name=f"s5_scan_{l}")
        ys = glu_gate(y, proj, full["w_glu"][l], b_glu[l][None], tl=512, name=f"glu_gate_{l}")
        (q, qt), (k, kt), (v, vt) = (to_heads(proj[:, b * W:(b + 1) * W] * s) for b, s in ((2, HEAD_DIM ** -0.5), (3, 1.0), (4, 1.0)))
        o = from_heads_t(attn_fwd(q, k, vt, tq=ATTN_TILE, nk=ATTN_SPAN, name=f"attn_fwd_{l}"))
        h_next, out = merge_out(ys, o, proj, h, full["w_branch_ssm"][l], full["w_branch_attn"][l], full["w_out"][l],
                                post_norm_g[l][None], tl=256, name=f"merge_out_{l}")
        saved.append(dict(x=h, prm=prm, proj=proj, y=y, states=states, ys=ys, q=q, k=k, v=v, qt=qt, kt=kt, o=o, out=out))
        h = h_next

    dx, loss_part = loss_grad(h, loss_target[0], tl=512, name="loss_grad")
    loss = lax.psum(loss_part[0, 0], MESH_AXES)

    grads = {n: [None] * depth for n in WEIGHTS}
    for l in reversed(range(depth)):
        s = saved[l]
        prm = s["prm"]
        dys, do, dtail, grads["w_out"][l], grads["w_branch_ssm"][l], grads["w_branch_attn"][l], dg_post = merge_out_bwd(
            dx, s["out"], s["ys"], s["o"], s["proj"], full["w_branch_ssm"][l], full["w_branch_attn"][l], full["w_out"][l],
            post_norm_g[l][None], tl=256, name=f"merge_out_bwd_{l}")
        do, do_t = to_heads(do)
        dq, dk, dv = attn_bwd(s["q"], s["k"], s["v"], do, s["qt"], s["kt"], do_t, tq=ATTN_TILE, nk=ATTN_SPAN, name=f"attn_bwd_{l}")
        dy, dz_ssm, grads["w_glu"][l], db_glu = glu_gate_bwd(dys, s["y"], s["proj"], full["w_glu"][l], b_glu[l][None], tl=512,
                                                             name=f"glu_gate_bwd_{l}")
        du, *s5_outs = s5_scan_bwd(dy, s["proj"], s["states"], prm["bb_bd"], prm["c_bd"], prm["abar_re"], prm["abar_im"], prm["d"],
                                   tl=256, name=f"s5_scan_bwd_{l}")
        for n, g in s5_param_grads(prm, *s5_outs, name=f"s5_disc_bwd_{l}").items():
            grads["ssm_" + n][l] = g
        dproj = jnp.concatenate([du, dz_ssm, from_heads_t(dq), from_heads_t(dk), from_heads_t(dv), dtail], axis=1)
        dx, dg_pre = norm_proj_bwd_x(dproj, s["x"], pre_norm_g[l][None], full["w_in"][l], dx, tl=256, name=f"norm_proj_bwd_x_{l}")
        grads["w_in"][l] = norm_proj_bwd_w(dproj, s["x"], pre_norm_g[l][None], tl=512, tn=w_in.shape[2], name=f"norm_proj_bwd_w_{l}")
        grads["pre_norm_g"][l], grads["post_norm_g"][l], grads["b_glu"][l] = dg_pre[0], dg_post[0], db_glu[0]

    def column_blocks(g):
        return g.reshape(g.shape[0], N_DEV, -1).transpose(1, 0, 2)

    parts = [jnp.stack(grads["w_in"], axis=1)]
    parts += [jnp.stack([column_blocks(g) for g in grads[n]], axis=1) for n in COLUMN_SHARDED[1:]]
    parts.append(jnp.stack([g.reshape(N_DEV, -1, D) for g in grads["w_out"]], axis=1))
    received = exchange_blocks([p.astype(WIRE_DTYPE) for p in parts], name="exchange_weight_grads")
    results = {}
    for n, r in zip((*COLUMN_SHARDED, "w_out"), received):
        shard = weights[n]
        C = shard.shape[-1]
        as_rows = lambda a: a.reshape(-1, C)
        rows = as_rows(shard).shape[0]
        outs = adamw_from_parts(r.reshape(N_DEV, rows, C), as_rows(shard), as_rows(given["m_" + n]), as_rows(given["v_" + n]),
                                tr=_row_tile(rows), name=f"adamw_{n}")
        results[n] = [o.reshape(shard.shape) for o in outs]

    small = [weights[n] for n in SMALL]
    packed = _pack([jnp.stack(grads[n]) for n in SMALL])
    (everyones,) = all_gather([packed], name="gather_small_grads")
    outs = adamw_from_parts(everyones, _pack(small), _pack([given["m_" + n] for n in SMALL]), _pack([given["v_" + n] for n in SMALL]),
                            tr=PACK_ROWS, name="adamw_small")
    for n, *four in zip(SMALL, *(_unpack(o, small) for o in outs)):
        results[n] = four

    return (loss, dx[None], *(results[n][0] for n in WEIGHTS), *(results[n][1] for n in WEIGHTS),
            *(results[n][2] for n in WEIGHTS), *(results[n][3] for n in WEIGHTS))
```

```python
import functools

import jax
import jax.numpy as jnp
from jax import lax
from jax.experimental import pallas as pl
from jax.experimental.pallas import tpu as pltpu

F32 = jnp.float32
MXU_DTYPE = jnp.bfloat16
WIRE_DTYPE = jnp.bfloat16
ATTN_TILE = 256
ATTN_SPAN = 4

EPS = 1e-6
N_HEADS = 8
HEAD_DIM = 64
ADAM_LR = 0.001
ADAM_B1 = 0.9
ADAM_B2 = 0.999
ADAM_EPS = 1e-08
ADAM_WD = 0.01
ADAM_STEP = 10

N_DEV = 8
MESH_AXES = ("x", "y", "c")
VMEM_LIMIT_BYTES = 56 * 1024 * 1024
MESH = pl.DeviceIdType.MESH
ANY = pl.BlockSpec(memory_space=pl.ANY)
WHOLE_VMEM = pl.BlockSpec(memory_space=pltpu.VMEM)


def _params(n_grid_axes):
    return pltpu.CompilerParams(dimension_semantics=("arbitrary",) * n_grid_axes, vmem_limit_bytes=VMEM_LIMIT_BYTES)


def _dot(a, b):
    return jnp.dot(a.astype(MXU_DTYPE), b.astype(MXU_DTYPE), preferred_element_type=F32)


def _dot_nt(a, b):
    return lax.dot_general(a.astype(MXU_DTYPE), b.astype(MXU_DTYPE), (((1,), (1,)), ((), ())), preferred_element_type=F32)


def _dot_tn(a, b):
    return lax.dot_general(a.astype(MXU_DTYPE), b.astype(MXU_DTYPE), (((0,), (0,)), ((), ())), preferred_element_type=F32)


def _rms(v, g):
    return v * lax.rsqrt(jnp.mean(v * v, axis=-1, keepdims=True) + EPS) * g


def norm_proj(x, g, w, *, tl, tn, name):
    L, D = x.shape
    N = w.shape[1]

    def body(x_ref, g_ref, w_ref, o_ref, ht_ref, h_scr):
        @pl.when(pl.program_id(1) == 0)
        def _():
            h = _rms(x_ref[...], g_ref[...])
            h_scr[...] = h.astype(MXU_DTYPE)
            ht_ref[...] = h.T.astype(MXU_DTYPE)

        o_ref[...] = jnp.dot(h_scr[...], w_ref[...], preferred_element_type=F32)

    return pl.pallas_call(
        body, name=name, grid=(L // tl, N // tn),
        in_specs=[pl.BlockSpec((tl, D), lambda i, j: (i, 0)), pl.BlockSpec((1, D), lambda i, j: (0, 0)),
                  pl.BlockSpec((D, tn), lambda i, j: (0, j))],
        out_specs=[pl.BlockSpec((tl, tn), lambda i, j: (i, j)), pl.BlockSpec((D, tl), lambda i, j: (0, i))],
        out_shape=[jax.ShapeDtypeStruct((L, N), F32), jax.ShapeDtypeStruct((D, L), MXU_DTYPE)],
        scratch_shapes=[pltpu.VMEM((tl, D), MXU_DTYPE)],
        compiler_params=_params(2),
    )(x, g, w)


def norm_proj_bwd_x(dproj, x, g, w, dxn, *, tl, name):
    L, D = x.shape
    N = w.shape[1]

    def body(dp_ref, x_ref, g_ref, w_ref, dxn_ref, dx_ref, dg_ref):
        @pl.when(pl.program_id(0) == 0)
        def _():
            dg_ref[...] = jnp.zeros_like(dg_ref)

        dh = _dot_nt(dp_ref[...], w_ref[...])
        _, vjp = jax.vjp(_rms, x_ref[...], g_ref[...])
        dx, dg = vjp(dh)
        dx_ref[...] = dxn_ref[...] + dx
        dg_ref[...] += dg

    return pl.pallas_call(
        body, name=name, grid=(L // tl,),
        in_specs=[pl.BlockSpec((tl, N), lambda i: (i, 0)), pl.BlockSpec((tl, D), lambda i: (i, 0)),
                  pl.BlockSpec((1, D), lambda i: (0, 0)), WHOLE_VMEM, pl.BlockSpec((tl, D), lambda i: (i, 0))],
        out_specs=[pl.BlockSpec((tl, D), lambda i: (i, 0)), pl.BlockSpec((1, D), lambda i: (0, 0))],
        out_shape=[jax.ShapeDtypeStruct((L, D), F32), jax.ShapeDtypeStruct((1, D), F32)],
        compiler_params=_params(1),
    )(dproj, x, g, w, dxn)


def norm_proj_bwd_w(dproj, ht, *, tl, tn, name):
    D, L = ht.shape
    N = dproj.shape[1]

    def body(dp_ref, ht_ref, dw_ref):
        @pl.when(pl.program_id(1) == 0)
        def _():
            dw_ref[...] = jnp.zeros_like(dw_ref)

        dw_ref[...] += _dot(ht_ref[...], dp_ref[...])

    return pl.pallas_call(
        body, name=name, grid=(N // tn, L // tl),
        in_specs=[pl.BlockSpec((tl, tn), lambda j, i: (i, j)), pl.BlockSpec((D, tl), lambda j, i: (0, i))],
        out_specs=pl.BlockSpec((None, D, tn), lambda j, i: (j, 0, 0)),
        out_shape=jax.ShapeDtypeStruct((N // tn, D, tn), F32),
        compiler_params=_params(2),
    )(dproj, ht)


def _s5_disc(a_re, a_im, log_dt, b_re, b_im):
    dt = jnp.exp(log_dt)
    mag = jnp.exp(a_re * dt)
    abar_re = mag * jnp.cos(a_im * dt)
    abar_im = mag * jnp.sin(a_im * dt)
    nr = abar_re - 1.0
    ni = abar_im
    den = a_re * a_re + a_im * a_im
    f_re = (nr * a_re + ni * a_im) / den
    f_im = (ni * a_re - nr * a_im) / den
    bb_re = f_re[None] * b_re - f_im[None] * b_im
    bb_im = f_re[None] * b_im + f_im[None] * b_re
    return abar_re, abar_im, bb_re, bb_im


def s5_disc(a_re, a_im, log_dt, b_re, b_im, *, name):
    def body(a_re_ref, a_im_ref, dt_ref, b_re_ref, b_im_ref, *outs):
        vals = _s5_disc(a_re_ref[...], a_im_ref[...], dt_ref[...], b_re_ref[...], b_im_ref[...])
        for o_ref, val in zip(outs, vals):
            o_ref[...] = val

    sd = jax.ShapeDtypeStruct
    return pl.pallas_call(
        body, name=name,
        out_shape=[sd(a_re.shape, F32), sd(a_re.shape, F32), sd(b_re.shape, F32), sd(b_re.shape, F32)],
    )(a_re, a_im, log_dt, b_re, b_im)


def s5_disc_bwd(a_re, a_im, log_dt, b_re, b_im, d_abar_re, d_abar_im, d_bb_re, d_bb_im, *, name):
    def body(a_re_ref, a_im_ref, dt_ref, b_re_ref, b_im_ref, c0, c1, c2, c3, *outs):
        _, vjp = jax.vjp(_s5_disc, a_re_ref[...], a_im_ref[...], dt_ref[...], b_re_ref[...], b_im_ref[...])
        grads = vjp((c0[...], c1[...], c2[...], c3[...]))
        for o_ref, val in zip(outs, grads):
            o_ref[...] = val

    sd = jax.ShapeDtypeStruct
    return pl.pallas_call(
        body, name=name,
        out_shape=[sd(a_re.shape, F32), sd(a_re.shape, F32), sd(log_dt.shape, F32), sd(b_re.shape, F32), sd(b_re.shape, F32)],
    )(a_re, a_im, log_dt, b_re, b_im, d_abar_re, d_abar_im, d_bb_re, d_bb_im)


def _diag_halves(W, S):
    H, Q = W // 2, S // 2
    return [(slice(k * H, (k + 1) * H), (slice(k * Q, (k + 1) * Q), slice(S + k * Q, S + (k + 1) * Q))) for k in range(2)]


def s5_scan(proj, bb_bd, c_bd, abar_re, abar_im, d_skip, *, tl, name):
    L = proj.shape[0]
    W, S2 = bb_bd.shape
    S = S2 // 2

    def body(u_ref, bb_ref, c_ref, ar_ref, ai_ref, d_ref, y_ref, xs_ref, sr_scr, si_scr):
        @pl.when(pl.program_id(0) == 0)
        def _():
            sr_scr[...] = jnp.zeros_like(sr_scr)
            si_scr[...] = jnp.zeros_like(si_scr)

        u = u_ref[...]
        for rows, col_ranges in _diag_halves(W, S):
            for cols in col_ranges:
                xs_ref[:, cols] = _dot(u[:, rows], bb_ref[rows, cols])
        ar = ar_ref[...]
        ai = ai_ref[...]

        def step(t, carry):
            sr, si = carry
            row = pl.ds(t, 1)
            nsr = ar * sr - ai * si + xs_ref[row, pl.ds(0, S)]
            nsi = ar * si + ai * sr + xs_ref[row, pl.ds(S, S)]
            xs_ref[row, pl.ds(0, S)] = nsr
            xs_ref[row, pl.ds(S, S)] = nsi
            return nsr, nsi

        sr, si = lax.fori_loop(0, tl, step, (sr_scr[...], si_scr[...]), unroll=8)
        sr_scr[...] = sr
        si_scr[...] = si
        y = [sum(_dot(xs_ref[:, cols], c_ref[cols, rows]) for cols in col_ranges) for rows, col_ranges in _diag_halves(W, S)]
        y_ref[...] = jnp.concatenate(y, axis=1) + d_ref[...] * u

    return pl.pallas_call(
        body, name=name, grid=(L // tl,),
        in_specs=[pl.BlockSpec((tl, W), lambda i: (i, 0)), WHOLE_VMEM, WHOLE_VMEM, WHOLE_VMEM, WHOLE_VMEM, WHOLE_VMEM],
        out_specs=[pl.BlockSpec((tl, W), lambda i: (i, 0)), pl.BlockSpec((tl, S2), lambda i: (i, 0))],
        out_shape=[jax.ShapeDtypeStruct((L, W), F32), jax.ShapeDtypeStruct((L, S2), F32)],
        scratch_shapes=[pltpu.VMEM((1, S), F32), pltpu.VMEM((1, S), F32)],
        compiler_params=_params(1),
    )(proj, bb_bd, c_bd, abar_re, abar_im, d_skip)


def s5_scan_bwd(dy, proj, xs, bb_bd, c_bd, abar_re, abar_im, d_skip, *, tl, name):
    L = proj.shape[0]
    W, S2 = bb_bd.shape
    S = S2 // 2
    nt = L // tl
    HALO = 8

    def body(dy_ref, u_ref, xs_ref, xprev_ref, bb_ref, c_ref, ar_ref, ai_ref, d_ref,
             du_ref, dd_ref, dar_ref, dai_ref, dbb_hbm, dc_hbm, lam_scr, lr_scr, li_scr, dbb_scr, dc_scr, sem):
        i = pl.program_id(0)

        @pl.when(i == 0)
        def _():
            for ref in (lr_scr, li_scr, dbb_scr, dc_scr, dd_ref, dar_ref, dai_ref):
                ref[...] = jnp.zeros_like(ref)

        dy = dy_ref[...]
        u = u_ref[...]
        xs_v = xs_ref[...]
        halves = _diag_halves(W, S)
        for rows, col_ranges in halves:
            for cols in col_ranges:
                dc_scr[cols, rows] += _dot_tn(xs_ref[:, cols], dy[:, rows])
                lam_scr[:, cols] = _dot_nt(dy[:, rows], c_ref[cols, rows])
        ar = ar_ref[...]
        ai = ai_ref[...]

        def step(s, carry):
            lr, li = carry
            row = pl.ds(tl - 1 - s, 1)
            nlr = lam_scr[row, pl.ds(0, S)] + ar * lr + ai * li
            nli = lam_scr[row, pl.ds(S, S)] - ai * lr + ar * li
            lam_scr[row, pl.ds(0, S)] = nlr
            lam_scr[row, pl.ds(S, S)] = nli
            return nlr, nli

        lr, li = lax.fori_loop(0, tl, step, (lr_scr[...], li_scr[...]), unroll=8)
        lr_scr[...] = lr
        li_scr[...] = li
        lam = lam_scr[...]
        before_tile = jnp.where(i == nt - 1, 0.0, xprev_ref[pl.ds(HALO - 1, 1), :])
        first_row = lax.broadcasted_iota(jnp.int32, (tl, 1), 0) == 0
        x_prev = jnp.where(first_row, before_tile, pltpu.roll(xs_v, 1, 0))
        lam_r, lam_i = lam[:, :S], lam[:, S:]
        xp_r, xp_i = x_prev[:, :S], x_prev[:, S:]
        dar_ref[...] += jnp.sum(lam_r * xp_r + lam_i * xp_i, axis=0, keepdims=True)
        dai_ref[...] += jnp.sum(lam_i * xp_r - lam_r * xp_i, axis=0, keepdims=True)
        du = [sum(_dot_nt(lam_scr[:, cols], bb_ref[rows, cols]) for cols in col_ranges) for rows, col_ranges in halves]
        du_ref[...] = dy * d_ref[...] + jnp.concatenate(du, axis=1)
        for rows, col_ranges in halves:
            for cols in col_ranges:
                dbb_scr[rows, cols] += _dot_tn(u[:, rows], lam_scr[:, cols])
        dd_ref[...] += jnp.sum(dy * u, axis=0, keepdims=True)

        @pl.when(i == nt - 1)
        def _():
            for src, dst in ((dbb_scr, dbb_hbm), (dc_scr, dc_hbm)):
                cp = pltpu.make_async_copy(src, dst, sem)
                cp.start()
                cp.wait()

    rev = lambda i: (nt - 1 - i, 0)
    const = lambda i: (0, 0)
    halo = lambda i: (jnp.maximum((nt - 1 - i) * (tl // HALO) - 1, 0), 0)
    sd = jax.ShapeDtypeStruct
    return pl.pallas_call(
        body, name=name, grid=(nt,),
        in_specs=[pl.BlockSpec((tl, W), rev), pl.BlockSpec((tl, W), rev), pl.BlockSpec((tl, S2), rev),
                  pl.BlockSpec((HALO, S2), halo), WHOLE_VMEM, WHOLE_VMEM, WHOLE_VMEM, WHOLE_VMEM, WHOLE_VMEM],
        out_specs=[pl.BlockSpec((tl, W), rev), pl.BlockSpec((1, W), const), pl.BlockSpec((1, S), const),
                   pl.BlockSpec((1, S), const), ANY, ANY],
        out_shape=[sd((L, W), F32), sd((1, W), F32), sd((1, S), F32), sd((1, S), F32), sd((W, S2), F32), sd((S2, W), F32)],
        scratch_shapes=[pltpu.VMEM((tl, S2), F32), pltpu.VMEM((1, S), F32), pltpu.VMEM((1, S), F32),
                        pltpu.VMEM((W, S2), F32), pltpu.VMEM((S2, W), F32), pltpu.SemaphoreType.DMA],
        compiler_params=_params(1),
    )(dy, proj, xs, xs, bb_bd, c_bd, abar_re, abar_im, d_skip)


def _glu_gate(gl, z):
    half = gl.shape[1] // 2
    return gl[:, :half] * jax.nn.sigmoid(gl[:, half:]) * jax.nn.silu(z)


def glu_gate(y, proj, w, b, *, tl, name):
    L, W = y.shape

    def body(y_ref, z_ref, w_ref, b_ref, o_ref):
        gl = _dot(jax.nn.gelu(y_ref[...]), w_ref[...]) + b_ref[...]
        o_ref[...] = _glu_gate(gl, z_ref[...])

    return pl.pallas_call(
        body, name=name, grid=(L // tl,),
        in_specs=[pl.BlockSpec((tl, W), lambda i: (i, 0)), pl.BlockSpec((tl, W), lambda i: (i, 1)), WHOLE_VMEM, WHOLE_VMEM],
        out_specs=pl.BlockSpec((tl, W), lambda i: (i, 0)),
        out_shape=jax.ShapeDtypeStruct((L, W), F32),
        compiler_params=_params(1),
    )(y, proj, w, b)


def glu_gate_bwd(dys, y, proj, w, b, *, tl, name):
    L, W = y.shape

    def body(dys_ref, y_ref, z_ref, w_ref, b_ref, dy_ref, dz_ref, dw_ref, db_ref):
        @pl.when(pl.program_id(0) == 0)
        def _():
            dw_ref[...] = jnp.zeros_like(dw_ref)
            db_ref[...] = jnp.zeros_like(db_ref)

        yg, gelu_vjp = jax.vjp(jax.nn.gelu, y_ref[...])
        gl = _dot(yg, w_ref[...]) + b_ref[...]
        _, gate_vjp = jax.vjp(_glu_gate, gl, z_ref[...])
        dgl, dz = gate_vjp(dys_ref[...])
        dz_ref[...] = dz
        dy_ref[...] = gelu_vjp(_dot_nt(dgl, w_ref[...]))[0]
        dw_ref[...] += _dot_tn(yg, dgl)
        db_ref[...] += jnp.sum(dgl, axis=0, keepdims=True)

    row = lambda i: (i, 0)
    const = lambda i: (0, 0)
    sd = jax.ShapeDtypeStruct
    return pl.pallas_call(
        body, name=name, grid=(L // tl,),
        in_specs=[pl.BlockSpec((tl, W), row), pl.BlockSpec((tl, W), row), pl.BlockSpec((tl, W), lambda i: (i, 1)),
                  WHOLE_VMEM, WHOLE_VMEM],
        out_specs=[pl.BlockSpec((tl, W), row), pl.BlockSpec((tl, W), row), pl.BlockSpec((W, 2 * W), const),
                   pl.BlockSpec((1, 2 * W), const)],
        out_shape=[sd((L, W), F32), sd((L, W), F32), sd((W, 2 * W), F32), sd((1, 2 * W), F32)],
        compiler_params=_params(1),
    )(dys, y, proj, w, b)


def _logs(z, mask):
    t = jnp.minimum(z, 0.0)
    log_beta = t - jnp.log(1.0 + jnp.exp(t + t - z))
    log_1m = log_beta - z
    if mask is not None:
        log_1m = jnp.where(mask, log_1m, 0.0)
    hi = log_1m.astype(MXU_DTYPE)
    lo = (log_1m - hi.astype(F32)).astype(MXU_DTYPE)
    return log_beta, log_1m, jnp.concatenate([hi, lo], axis=1)


def _keys(t, tq):
    return pl.ds(pl.multiple_of(t * tq, tq), tq)


def _span(j, nk):
    return [j * nk + h for h in reversed(range(nk))]


def _prefetch(x_i, t_ref, j, nk, tq, scr):
    for h, t in enumerate(_span(jnp.maximum(j, 0), nk)):
        scr[h] = _dot(x_i, t_ref[:, _keys(t, tq)])


def columns_t(a, blocks, scales, *, width, tl, name):
    L = a.shape[0]
    n = len(blocks)

    def body(*refs):
        for a_ref, o_ref, s in zip(refs[:n], refs[n:], scales):
            o_ref[...] = (a_ref[...] * s).T.astype(MXU_DTYPE)

    return pl.pallas_call(
        body, name=name, grid=(L // tl,),
        in_specs=[pl.BlockSpec((tl, width), functools.partial(lambda b, i: (i, b), b)) for b in blocks],
        out_specs=[pl.BlockSpec((width, tl), lambda i: (0, i))] * n,
        out_shape=[jax.ShapeDtypeStruct((width, L), MXU_DTYPE)] * n,
        compiler_params=_params(1),
    )(*[a] * n)


def rows_from_t(arrays_t, *, tl, name):
    W, L = arrays_t[0].shape
    n = len(arrays_t)

    def body(*refs):
        for a_ref, o_ref in zip(refs[:n], refs[n:]):
            o_ref[...] = a_ref[...].T

    return pl.pallas_call(
        body, name=name, grid=(L // tl,),
        in_specs=[pl.BlockSpec((W, tl), lambda i: (0, i))] * n,
        out_specs=[pl.BlockSpec((tl, W), lambda i: (i, 0))] * n,
        out_shape=[jax.ShapeDtypeStruct((L, W), F32)] * n,
        compiler_params=_params(1),
    )(*arrays_t)


def _weights(z_scr, masks, R, tri2, between):
    logs = [_logs(z_scr[h], m) for h, m in enumerate(masks)]
    after = [jnp.dot(split, tri2, preferred_element_type=F32) for _, _, split in logs]
    between()
    w = []
    for (log_beta, log_1m, _), a, m in zip(logs, after, masks):
        wt = jnp.exp(log_beta + a + R)
        w.append(wt if m is None else jnp.where(m, wt, 0.0))
        R = R + jnp.sum(log_1m, axis=1, keepdims=True)
    return [lg[0] for lg in logs], w, R


def _iotas(t):
    return lax.broadcasted_iota(jnp.int32, (t, t), 0), lax.broadcasted_iota(jnp.int32, (t, t), 1)


def _suffix_tri2(row, col):
    tri = (row > col).astype(MXU_DTYPE)
    return jnp.concatenate([tri, tri], axis=0)


def _rows(t_ref):
    return t_ref[...].astype(F32).T.astype(MXU_DTYPE)


def attn_fwd(qt, kt, vt, *, tq, nk, name):
    nh, hd, L = qt.shape
    assert L % (tq * nk) == 0

    def body(qt_ref, kt_ref, vt_ref, o_ref, z_scr, w_scr):
        i = pl.program_id(1)
        q_i = _rows(qt_ref)
        row, col = _iotas(tq)
        tri2 = _suffix_tri2(row, col)
        J = i // nk

        def weights(j, R, diag):
            masks = [(col + t * tq < row + i * tq) if diag else None for t in _span(j, nk)]
            _, w, R = _weights(z_scr, masks, R, tri2, lambda: _prefetch(q_i, kt_ref, j - 1, nk, tq, z_scr))
            for h, wt in enumerate(w):
                w_scr[h] = wt.astype(w_scr.dtype)
            return R

        def flush(j, acc):
            for h, t in enumerate(_span(j, nk)):
                acc = acc + _dot_nt(vt_ref[:, _keys(t, tq)], w_scr[h])
            return acc

        _prefetch(q_i, kt_ref, J, nk, tq, z_scr)
        R = weights(J, jnp.zeros((tq, 1), F32), True)

        def step(s, carry):
            acc, R = carry
            j = J - 1 - s
            acc = flush(j + 1, acc)
            return acc, weights(j, R, False)

        acc, _ = lax.fori_loop(0, J, step, (jnp.zeros((hd, tq), F32), R))
        o_ref[...] = flush(0, acc)

    tile = pl.BlockSpec((None, hd, tq), lambda h, i: (h, 0, i))
    whole = pl.BlockSpec((None, hd, L), lambda h, i: (h, 0, 0))
    return pl.pallas_call(
        body, name=name, grid=(nh, L // tq),
        in_specs=[tile, whole, whole], out_specs=tile,
        out_shape=jax.ShapeDtypeStruct((nh, hd, L), F32),
        scratch_shapes=[pltpu.VMEM((nk, tq, tq), F32), pltpu.VMEM((nk, tq, tq), MXU_DTYPE)],
        compiler_params=_params(2),
    )(qt, kt, vt)


def attn_bwd(qt, kt, vt, dot, *, tq, nk, name):
    nh, hd, L = qt.shape
    assert L % (tq * nk) == 0
    nq = L // tq
    scale = hd ** -0.5

    def body(qt_ref, kt_ref, vt_ref, dot_ref, dq_ref, dk_ref, dv_ref, e_scr, beta_scr, z_scr, dw_scr, w_scr):
        i = pl.program_id(1)

        @pl.when(i == 0)
        def _():
            dk_ref[...] = jnp.zeros_like(dk_ref)
            dv_ref[...] = jnp.zeros_like(dv_ref)

        q_i = _rows(qt_ref)
        do_i = _rows(dot_ref)
        dot_i = dot_ref[...]
        qt_i = qt_ref[...]
        row, col = _iotas(tq)
        tri2 = _suffix_tri2(row, col)
        prefix_tri = (row < col).astype(MXU_DTYPE)
        J = i // nk

        def prefetch(j):
            _prefetch(q_i, kt_ref, j, nk, tq, z_scr)
            _prefetch(do_i, vt_ref, j, nk, tq, dw_scr)

        def rebuild(j, R, diag):
            tiles = _span(j, nk)
            masks = [(col + t * tq < row + i * tq) if diag else None for t in tiles]
            dw = [dw_scr[h] for h in range(nk)]
            log_beta, w, R = _weights(z_scr, masks, R, tri2, lambda: prefetch(j - 1))
            for h, (t, lb, wt, dwt, m) in enumerate(zip(tiles, log_beta, w, dw, masks)):
                w_scr[h] = wt.astype(w_scr.dtype)
                e_scr[t] = dwt * wt
                beta = jnp.exp(lb)
                beta_scr[t] = (beta if m is None else jnp.where(m, beta, 0.0)).astype(beta_scr.dtype)
            return R

        def flush(j):
            for h, t in enumerate(_span(j, nk)):
                dv_ref[:, _keys(t, tq)] += _dot(dot_i, w_scr[h])

        prefetch(J)
        R = rebuild(J, jnp.zeros((tq, 1), F32), True)

        def step(s, R):
            j = J - 1 - s
            flush(j + 1)
            return rebuild(j, R, False)

        lax.fori_loop(0, J, step, R)
        flush(0)

        def push(j, carry):
            dq, P = carry
            tiles = [j * nk + h for h in range(nk)]
            e = [e_scr[t] for t in tiles]
            before = [_dot(ev, prefix_tri) for ev in e]
            for t, ev, bv in zip(tiles, e, before):
                dz = (ev - beta_scr[t].astype(F32) * (ev + bv + P)).astype(MXU_DTYPE)
                dk_ref[:, _keys(t, tq)] += _dot(qt_i, dz)
                dq = dq + _dot_nt(kt_ref[:, _keys(t, tq)], dz)
                P = P + jnp.sum(ev, axis=1, keepdims=True)
            return dq, P

        dq, _ = lax.fori_loop(0, J + 1, push, (jnp.zeros((hd, tq), F32), jnp.zeros((tq, 1), F32)))
        dq_ref[...] = dq * scale

    tile = pl.BlockSpec((None, hd, tq), lambda h, i: (h, 0, i))
    whole = pl.BlockSpec((None, hd, L), lambda h, i: (h, 0, 0))
    sd = jax.ShapeDtypeStruct((nh, hd, L), F32)
    return pl.pallas_call(
        body, name=name, grid=(nh, nq),
        in_specs=[tile, whole, whole, tile], out_specs=[tile, whole, whole],
        out_shape=[sd, sd, sd],
        scratch_shapes=[pltpu.VMEM((nq, tq, tq), F32), pltpu.VMEM((nq, tq, tq), MXU_DTYPE), pltpu.VMEM((nk, tq, tq), F32),
                        pltpu.VMEM((nk, tq, tq), F32), pltpu.VMEM((nk, tq, tq), MXU_DTYPE)],
        compiler_params=_params(2),
    )(qt, kt, vt, dot)


def _block_diag(blocks):
    G, m, n = blocks.shape
    eye = jnp.eye(G, dtype=blocks.dtype)
    return (blocks[:, :, None, :] * eye[:, None, :, None]).reshape(G * m, G * n)


def _diag_blocks(dense, m, n):
    G = dense.shape[0] // m
    return jnp.einsum("gmgn->gmn", dense.reshape(G, m, G, n))


def s5_prepare(a_re, a_im, log_dt, b_re, b_im, c_re, c_im, d_skip, *, name):
    G, P = a_re.shape
    b_re_t = b_re.transpose(2, 0, 1)
    b_im_t = b_im.transpose(2, 0, 1)
    log_dt = log_dt.reshape(G, 1)
    abar_re, abar_im, bb_re, bb_im = s5_disc(a_re, a_im, log_dt, b_re_t, b_im_t, name=name)
    bb_bd = jnp.concatenate([_block_diag(bb_re.transpose(1, 0, 2)), _block_diag(bb_im.transpose(1, 0, 2))], axis=1)
    c_bd = jnp.concatenate([_block_diag(c_re.transpose(0, 2, 1)), -_block_diag(c_im.transpose(0, 2, 1))], axis=0)
    return dict(a_re=a_re, a_im=a_im, log_dt=log_dt, b_re_t=b_re_t, b_im_t=b_im_t,
                abar_re=abar_re.reshape(1, G * P), abar_im=abar_im.reshape(1, G * P),
                bb_bd=bb_bd.astype(MXU_DTYPE), c_bd=c_bd.astype(MXU_DTYPE), d=d_skip.reshape(1, -1))


def s5_param_grads(prm, dd, dar, dai, dbb_bd, dc_bd, *, name):
    G, P = prm["a_re"].shape
    H = prm["b_re_t"].shape[0]
    S = G * P
    d_bb_re = _diag_blocks(dbb_bd[:, :S], H, P).transpose(1, 0, 2)
    d_bb_im = _diag_blocks(dbb_bd[:, S:], H, P).transpose(1, 0, 2)
    da_re, da_im, dlog_dt, db_re_t, db_im_t = s5_disc_bwd(
        prm["a_re"], prm["a_im"], prm["log_dt"], prm["b_re_t"], prm["b_im_t"],
        dar.reshape(G, P), dai.reshape(G, P), d_bb_re, d_bb_im, name=name)
    return dict(a_re=da_re, a_im=da_im, log_dt=dlog_dt.reshape(G), b_re=db_re_t.transpose(1, 2, 0), b_im=db_im_t.transpose(1, 2, 0),
                c_re=_diag_blocks(dc_bd[:S], P, H).transpose(0, 2, 1), c_im=-_diag_blocks(dc_bd[S:], P, H).transpose(0, 2, 1),
                d=dd.reshape(-1))


def _silu_gate(o, z):
    return o * jax.nn.silu(z)


def _merge(g_ssm, p_ssm, g_attn, p_attn):
    return jax.nn.sigmoid(g_ssm) * p_ssm + jax.nn.sigmoid(g_attn) * p_attn


def _proj_gates(tl, W, D):
    return [pl.BlockSpec((tl, W), lambda i: (i, 5)), pl.BlockSpec((tl, D), lambda i: (i, 3)), pl.BlockSpec((tl, D), lambda i: (i, 4))]


def merge_out(ys, o, proj, x, w_bs, w_ba, w_out, g_post, *, tl, name):
    L, W = ys.shape
    D = x.shape[1]

    def body(ys_ref, o_ref, z_ref, gs_ref, ga_ref, x_ref, wbs_ref, wba_ref, wout_ref, g_ref, xn_ref, out_ref):
        ya = _silu_gate(o_ref[...], z_ref[...])
        merged = _merge(gs_ref[...], _dot(ys_ref[...], wbs_ref[...]), ga_ref[...], _dot(ya, wba_ref[...]))
        out = _dot(merged, wout_ref[...])
        out_ref[...] = out
        xn_ref[...] = x_ref[...] + _rms(out, g_ref[...])

    row = lambda i: (i, 0)
    sd = jax.ShapeDtypeStruct
    return pl.pallas_call(
        body, name=name, grid=(L // tl,),
        in_specs=[pl.BlockSpec((tl, W), row), pl.BlockSpec((tl, W), row), *_proj_gates(tl, W, D), pl.BlockSpec((tl, D), row),
                  WHOLE_VMEM, WHOLE_VMEM, WHOLE_VMEM, WHOLE_VMEM],
        out_specs=[pl.BlockSpec((tl, D), row), pl.BlockSpec((tl, D), row)],
        out_shape=[sd((L, D), F32), sd((L, D), F32)],
        compiler_params=_params(1),
    )(ys, o, proj, proj, proj, x, w_bs, w_ba, w_out, g_post)


def merge_out_bwd(dxn, out, ys, o, proj, w_bs, w_ba, w_out, g_post, *, tl, name):
    L, W = ys.shape
    D = dxn.shape[1]

    def body(dxn_ref, out_ref, ys_ref, o_ref, z_ref, gs_ref, ga_ref, wbs_ref, wba_ref, wout_ref, g_ref,
             dys_ref, do_ref, dtail_ref, dwout_ref, dwbs_ref, dwba_ref, dg_ref):
        @pl.when(pl.program_id(0) == 0)
        def _():
            for ref in (dwout_ref, dwbs_ref, dwba_ref, dg_ref):
                ref[...] = jnp.zeros_like(ref)

        _, rms_vjp = jax.vjp(_rms, out_ref[...], g_ref[...])
        dout, dg = rms_vjp(dxn_ref[...])
        dg_ref[...] += dg
        ys_v = ys_ref[...]
        ya, gate_vjp = jax.vjp(_silu_gate, o_ref[...], z_ref[...])
        merged, merge_vjp = jax.vjp(_merge, gs_ref[...], _dot(ys_v, wbs_ref[...]), ga_ref[...], _dot(ya, wba_ref[...]))
        dwout_ref[...] += _dot_tn(merged, dout)
        dgs, dps, dga, dpa = merge_vjp(_dot_nt(dout, wout_ref[...]))
        dys_ref[...] = _dot_nt(dps, wbs_ref[...])
        dwbs_ref[...] += _dot_tn(ys_v, dps)
        dwba_ref[...] += _dot_tn(ya, dpa)
        do, dz = gate_vjp(_dot_nt(dpa, wba_ref[...]))
        do_ref[...] = do
        dtail_ref[:, pl.ds(0, W)] = dz
        dtail_ref[:, pl.ds(W, D)] = dgs
        dtail_ref[:, pl.ds(W + D, D)] = dga

    row = lambda i: (i, 0)
    const = lambda i: (0, 0)
    sd = jax.ShapeDtypeStruct
    return pl.pallas_call(
        body, name=name, grid=(L // tl,),
        in_specs=[pl.BlockSpec((tl, D), row), pl.BlockSpec((tl, D), row), pl.BlockSpec((tl, W), row), pl.BlockSpec((tl, W), row),
                  *_proj_gates(tl, W, D), WHOLE_VMEM, WHOLE_VMEM, WHOLE_VMEM, WHOLE_VMEM],
        out_specs=[pl.BlockSpec((tl, W), row), pl.BlockSpec((tl, W), row), pl.BlockSpec((tl, W + 2 * D), row),
                   pl.BlockSpec((D, D), const), pl.BlockSpec((W, D), const), pl.BlockSpec((W, D), const), pl.BlockSpec((1, D), const)],
        out_shape=[sd((L, W), F32), sd((L, W), F32), sd((L, W + 2 * D), F32), sd((D, D), F32), sd((W, D), F32), sd((W, D), F32),
                   sd((1, D), F32)],
        compiler_params=_params(1),
    )(dxn, out, ys, o, proj, proj, proj, w_bs, w_ba, w_out, g_post)


def loss_grad(y, target, *, tl, name):
    L, D = y.shape

    def body(y_ref, t_ref, dy_ref, loss_ref):
        @pl.when(pl.program_id(0) == 0)
        def _():
            loss_ref[...] = jnp.zeros_like(loss_ref)

        err = y_ref[...] - t_ref[...]
        dy_ref[...] = err / D
        loss_ref[...] += 0.5 * jnp.sum(jnp.mean(err * err, axis=-1, keepdims=True), axis=0, keepdims=True)

    row = lambda i: (i, 0)
    return pl.pallas_call(
        body, name=name, grid=(L // tl,),
        in_specs=[pl.BlockSpec((tl, D), row), pl.BlockSpec((tl, D), row)],
        out_specs=[pl.BlockSpec((tl, D), row), pl.BlockSpec((1, 1), lambda i: (0, 0))],
        out_shape=[jax.ShapeDtypeStruct((L, D), F32), jax.ShapeDtypeStruct((1, 1), F32)],
        compiler_params=_params(1),
    )(y, target)


def adamw_from_parts(parts, w, m, v, *, tr, name):
    n, R, C = parts.shape

    def body(p_ref, w_ref, m_ref, v_ref, g_ref, d_ref, nm_ref, nv_ref):
        g = p_ref[0].astype(F32)
        for s in range(1, n):
            g = g + p_ref[s].astype(F32)
        nm = ADAM_B1 * m_ref[...] + (1.0 - ADAM_B1) * g
        nv = ADAM_B2 * v_ref[...] + (1.0 - ADAM_B2) * (g * g)
        m_hat = nm / (1.0 - ADAM_B1 ** ADAM_STEP)
        v_hat = nv / (1.0 - ADAM_B2 ** ADAM_STEP)
        g_ref[...] = g
        d_ref[...] = -ADAM_LR * (m_hat / (jnp.sqrt(v_hat) + ADAM_EPS) + ADAM_WD * w_ref[...])
        nm_ref[...] = nm
        nv_ref[...] = nv

    tile = pl.BlockSpec((tr, C), lambda i: (i, 0))
    sd = jax.ShapeDtypeStruct((R, C), F32)
    return pl.pallas_call(
        body, name=name, grid=(R // tr,),
        in_specs=[pl.BlockSpec((n, tr, C), lambda i: (0, i, 0)), tile, tile, tile],
        out_specs=[tile, tile, tile, tile], out_shape=[sd, sd, sd, sd],
        compiler_params=_params(1),
    )(parts, w, m, v)


PER_PEER = N_DEV - 1


def _position():
    return lax.axis_index("x"), lax.axis_index("y"), lax.axis_index("c")


def _flat(x, y, c):
    return 4 * x + 2 * y + c


def all_gather(shards, *, name):
    n = len(shards)

    def body(*refs):
        ins, outs = refs[:n], refs[n:2 * n]
        send_sems, recv_sems, local_sems = refs[2 * n:]
        x, y, c = _position()
        me, sibling = (x, y, c), (x, y, 1 - c)
        chips = [(1 - x, y), (x, 1 - y), (1 - x, 1 - y)]

        def slot(a, dev):
            return outs[a].at[_flat(*dev)]

        def copy(a, k, block, to, src=None):
            return pltpu.make_async_remote_copy(
                src_ref=slot(a, block) if src is None else src, dst_ref=slot(a, block),
                send_sem=send_sems.at[a * PER_PEER + k], recv_sem=recv_sems.at[a * PER_PEER + k],
                device_id=to, device_id_type=MESH)

        mine = [pltpu.make_async_copy(ins[a], slot(a, me), local_sems.at[a]) for a in range(n)]
        for cp in mine:
            cp.start()
        first = []
        for a in range(n):
            first.append(copy(a, 0, me, sibling, src=ins[a]))
            first += [copy(a, 1 + j, me, (*chip, c), src=ins[a]) for j, chip in enumerate(chips)]
        for cp in first:
            cp.start()
        passed = []
        for j, chip in enumerate(chips):
            for a in range(n):
                copy(a, 1 + j, (*chip, c), me).wait_recv()
                passed.append(copy(a, 4 + j, (*chip, c), sibling))
                passed[-1].start()
        for a in range(n):
            copy(a, 0, sibling, me).wait_recv()
            for j, chip in enumerate(chips):
                copy(a, 4 + j, (*chip, 1 - c), me).wait_recv()
        for cp in first + passed:
            cp.wait_send()
        for cp in mine:
            cp.wait()

    return pl.pallas_call(
        body, name=name,
        in_specs=[ANY] * n, out_specs=[ANY] * n,
        out_shape=[jax.ShapeDtypeStruct((N_DEV, *s.shape), s.dtype) for s in shards],
        scratch_shapes=[pltpu.SemaphoreType.DMA((n * PER_PEER,)), pltpu.SemaphoreType.DMA((n * PER_PEER,)),
                        pltpu.SemaphoreType.DMA((n,))],
    )(*shards)


def exchange_blocks(parts, *, name):
    n = len(parts)

    def body(*refs):
        ins, outs = refs[:n], refs[n:2 * n]
        send_sems, recv_sems, local_sems = refs[2 * n:]
        x, y, c = _position()
        me = _flat(x, y, c)
        peers = []
        for k in range(1, N_DEV):
            peer = (1 - x if k & 4 else x, 1 - y if k & 2 else y, 1 - c if k & 1 else c)
            peers.append((k - 1, peer, _flat(*peer)))

        mine = [pltpu.make_async_copy(ins[a].at[me], outs[a].at[me], local_sems.at[a]) for a in range(n)]
        for cp in mine:
            cp.start()
        sends = [pltpu.make_async_remote_copy(
            src_ref=ins[a].at[peer_flat], dst_ref=outs[a].at[me],
            send_sem=send_sems.at[a * PER_PEER + k], recv_sem=recv_sems.at[a * PER_PEER + k],
            device_id=peer, device_id_type=MESH) for a in range(n) for k, peer, peer_flat in peers]
        for cp in sends:
            cp.start()
        for a in range(n):
            for k, peer, peer_flat in peers:
                pltpu.make_async_remote_copy(
                    src_ref=ins[a].at[me], dst_ref=outs[a].at[peer_flat],
                    send_sem=send_sems.at[a * PER_PEER + k], recv_sem=recv_sems.at[a * PER_PEER + k],
                    device_id=peer, device_id_type=MESH).wait_recv()
        for cp in sends:
            cp.wait_send()
        for cp in mine:
            cp.wait()

    return pl.pallas_call(
        body, name=name,
        in_specs=[ANY] * n, out_specs=[ANY] * n,
        out_shape=[jax.ShapeDtypeStruct(p.shape, p.dtype) for p in parts],
        scratch_shapes=[pltpu.SemaphoreType.DMA((n * PER_PEER,)), pltpu.SemaphoreType.DMA((n * PER_PEER,)),
                        pltpu.SemaphoreType.DMA((n,))],
    )(*parts)


LANES = 128
SMALL = ("pre_norm_g", "post_norm_g", "ssm_a_re", "ssm_a_im", "ssm_log_dt", "ssm_b_re", "ssm_b_im", "ssm_c_re", "ssm_c_im",
         "ssm_d", "b_glu")
COLUMN_SHARDED = ("w_in", "w_glu", "w_branch_ssm", "w_branch_attn")
WEIGHTS = ("pre_norm_g", "post_norm_g", "w_in", "ssm_a_re", "ssm_a_im", "ssm_log_dt", "ssm_b_re", "ssm_b_im", "ssm_c_re",
           "ssm_c_im", "ssm_d", "w_glu", "b_glu", "w_branch_ssm", "w_branch_attn", "w_out")
PACK_ROWS = 256


def _pack(arrays):
    flat = jnp.concatenate([a.reshape(-1) for a in arrays])
    tile = PACK_ROWS * LANES
    return jnp.pad(flat, (0, -flat.size % tile)).reshape(-1, LANES)


def _unpack(packed, like):
    flat, out, at = packed.reshape(-1), [], 0
    for a in like:
        out.append(flat[at:at + a.size].reshape(a.shape))
        at += a.size
    return out


def _row_tile(rows):
    return next(t for t in (256, 128, 64, 32, 16, 8) if rows % t == 0)


def kernel(x, pre_norm_g, post_norm_g, w_in, ssm_a_re, ssm_a_im, ssm_log_dt, ssm_b_re, ssm_b_im, ssm_c_re, ssm_c_im, ssm_d, w_glu, b_glu, w_branch_ssm, w_branch_attn, w_out, loss_target, m_pre_norm_g, m_post_norm_g, m_w_in, m_ssm_a_re, m_ssm_a_im, m_ssm_log_dt, m_ssm_b_re, m_ssm_b_im, m_ssm_c_re, m_ssm_c_im, m_ssm_d, m_w_glu, m_b_glu, m_w_branch_ssm, m_w_branch_attn, m_w_out, v_pre_norm_g, v_post_norm_g, v_w_in, v_ssm_a_re, v_ssm_a_im, v_ssm_log_dt, v_ssm_b_re, v_ssm_b_im, v_ssm_c_re, v_ssm_c_im, v_ssm_d, v_w_glu, v_b_glu, v_w_branch_ssm, v_w_branch_attn, v_w_out):
    given = dict(locals())
    weights = {n: given[n] for n in WEIGHTS}
    depth, D = pre_norm_g.shape
    h = x[0]
    L = h.shape[0]
    W = w_glu.shape[1]
    by_head = (N_HEADS, HEAD_DIM, L)

    gathered = all_gather([weights[n].astype(MXU_DTYPE) for n in (*COLUMN_SHARDED, "w_out")], name="gather_weights")
    full = {n: g.transpose(1, 2, 0, 3).reshape(depth, g.shape[2], -1) for n, g in zip(COLUMN_SHARDED, gathered)}
    full["w_out"] = gathered[-1].transpose(1, 0, 2, 3).reshape(depth, -1, D)

    saved = []
    for l in range(depth):
        prm = s5_prepare(ssm_a_re[l], ssm_a_im[l], ssm_log_dt[l], ssm_b_re[l], ssm_b_im[l], ssm_c_re[l], ssm_c_im[l], ssm_d[l],
                         name=f"s5_disc_{l}")
        proj, ht = norm_proj(h, pre_norm_g[l][None], full["w_in"][l], tl=512, tn=1280, name=f"norm_proj_{l}")
        y, states = s5_scan(proj, prm["bb_bd"], prm["c_bd"], prm["abar_re"], prm["abar_im"], prm["d"], tl=256, name=f"s5_scan_{l}")
        ys = glu_gate(y, proj, full["w_glu"][l], b_glu[l][None], tl=512, name=f"glu_gate_{l}")
        qt, kt, vt = (t.reshape(by_head) for t in columns_t(proj, (2, 3, 4), (HEAD_DIM ** -0.5, 1.0, 1.0), width=W, tl=512,
                                                            name=f"qkv_t_{l}"))
        (o,) = rows_from_t([attn_fwd(qt, kt, vt, tq=ATTN_TILE, nk=ATTN_SPAN, name=f"attn_fwd_{l}").reshape(W, L)], tl=512,
                           name=f"attn_out_{l}")
        h_next, out = merge_out(ys, o, proj, h, full["w_branch_ssm"][l], full["w_branch_attn"][l], full["w_out"][l],
                                post_norm_g[l][None], tl=256, name=f"merge_out_{l}")
        saved.append(dict(x=h, ht=ht, prm=prm, proj=proj, y=y, states=states, ys=ys, qt=qt, kt=kt, vt=vt, o=o, out=out))
        h = h_next

    dx, loss_part = loss_grad(h, loss_target[0], tl=512, name="loss_grad")
    loss = lax.psum(loss_part[0, 0], MESH_AXES)

    grads = {n: [None] * depth for n in WEIGHTS}
    for l in reversed(range(depth)):
        s = saved[l]
        prm = s["prm"]
        dys, do, dtail, grads["w_out"][l], grads["w_branch_ssm"][l], grads["w_branch_attn"][l], dg_post = merge_out_bwd(
            dx, s["out"], s["ys"], s["o"], s["proj"], full["w_branch_ssm"][l], full["w_branch_attn"][l], full["w_out"][l],
            post_norm_g[l][None], tl=256, name=f"merge_out_bwd_{l}")
        (do_t,) = columns_t(do, (0,), (1.0,), width=W, tl=512, name=f"do_t_{l}")
        dqkv_t = attn_bwd(s["qt"], s["kt"], s["vt"], do_t.reshape(by_head), tq=ATTN_TILE, nk=ATTN_SPAN, name=f"attn_bwd_{l}")
        dq, dk, dv = rows_from_t([t.reshape(W, L) for t in dqkv_t], tl=512, name=f"attn_grads_{l}")
        dy, dz_ssm, grads["w_glu"][l], db_glu = glu_gate_bwd(dys, s["y"], s["proj"], full["w_glu"][l], b_glu[l][None], tl=512,
                                                             name=f"glu_gate_bwd_{l}")
        du, *s5_outs = s5_scan_bwd(dy, s["proj"], s["states"], prm["bb_bd"], prm["c_bd"], prm["abar_re"], prm["abar_im"], prm["d"],
                                   tl=256, name=f"s5_scan_bwd_{l}")
        for n, g in s5_param_grads(prm, *s5_outs, name=f"s5_disc_bwd_{l}").items():
            grads["ssm_" + n][l] = g
        dproj = jnp.concatenate([du, dz_ssm, dq, dk, dv, dtail], axis=1)
        dx, dg_pre = norm_proj_bwd_x(dproj, s["x"], pre_norm_g[l][None], full["w_in"][l], dx, tl=256, name=f"norm_proj_bwd_x_{l}")
        grads["w_in"][l] = norm_proj_bwd_w(dproj, s["ht"], tl=512, tn=w_in.shape[2], name=f"norm_proj_bwd_w_{l}")
        grads["pre_norm_g"][l], grads["post_norm_g"][l], grads["b_glu"][l] = dg_pre[0], dg_post[0], db_glu[0]

    def column_blocks(g):
        return g.reshape(g.shape[0], N_DEV, -1).transpose(1, 0, 2)

    parts = [jnp.stack(grads["w_in"], axis=1)]
    parts += [jnp.stack([column_blocks(g) for g in grads[n]], axis=1) for n in COLUMN_SHARDED[1:]]
    parts.append(jnp.stack([g.reshape(N_DEV, -1, D) for g in grads["w_out"]], axis=1))
    received = exchange_blocks([p.astype(WIRE_DTYPE) for p in parts], name="exchange_weight_grads")
    results = {}
    for n, r in zip((*COLUMN_SHARDED, "w_out"), received):
        shard = weights[n]
        C = shard.shape[-1]
        as_rows = lambda a: a.reshape(-1, C)
        rows = as_rows(shard).shape[0]
        outs = adamw_from_parts(r.reshape(N_DEV, rows, C), as_rows(shard), as_rows(given["m_" + n]), as_rows(given["v_" + n]),
                                tr=_row_tile(rows), name=f"adamw_{n}")
        results[n] = [o.reshape(shard.shape) for o in outs]

    small = [weights[n] for n in SMALL]
    packed = _pack([jnp.stack(grads[n]) for n in SMALL])
    (everyones,) = all_gather([packed], name="gather_small_grads")
    outs = adamw_from_parts(everyones, _pack(small), _pack([given["m_" + n] for n in SMALL]), _pack([given["v_" + n] for n in SMALL]),
                            tr=PACK_ROWS, name="adamw_small")
    for n, *four in zip(SMALL, *(_unpack(o, small) for o in outs)):
        results[n] = four

    return (loss, dx[None], *(results[n][0] for n in WEIGHTS), *(results[n][1] for n in WEIGHTS),
            *(results[n][2] for n in WEIGHTS), *(results[n][3] for n in WEIGHTS))
```

```python
import functools

import jax
import jax.numpy as jnp
from jax import lax
from jax.experimental import pallas as pl
from jax.experimental.pallas import tpu as pltpu

F32 = jnp.float32
MXU_DTYPE = jnp.bfloat16
WIRE_DTYPE = jnp.bfloat16
ATTN_TILE = 256
ATTN_SPAN = 4

EPS = 1e-6
N_HEADS = 8
HEAD_DIM = 64
ADAM_LR = 0.001
ADAM_B1 = 0.9
ADAM_B2 = 0.999
ADAM_EPS = 1e-08
ADAM_WD = 0.01
ADAM_STEP = 10

N_DEV = 8
MESH_AXES = ("x", "y", "c")
VMEM_LIMIT_BYTES = 56 * 1024 * 1024
MESH = pl.DeviceIdType.MESH
ANY = pl.BlockSpec(memory_space=pl.ANY)
WHOLE_VMEM = pl.BlockSpec(memory_space=pltpu.VMEM)


def _params(n_grid_axes):
    return pltpu.CompilerParams(dimension_semantics=("arbitrary",) * n_grid_axes, vmem_limit_bytes=VMEM_LIMIT_BYTES)


def _dot(a, b):
    return jnp.dot(a.astype(MXU_DTYPE), b.astype(MXU_DTYPE), preferred_element_type=F32)


def _dot_nt(a, b):
    return lax.dot_general(a.astype(MXU_DTYPE), b.astype(MXU_DTYPE), (((1,), (1,)), ((), ())), preferred_element_type=F32)


def _dot_tn(a, b):
    return lax.dot_general(a.astype(MXU_DTYPE), b.astype(MXU_DTYPE), (((0,), (0,)), ((), ())), preferred_element_type=F32)


def _rms(v, g):
    return v * lax.rsqrt(jnp.mean(v * v, axis=-1, keepdims=True) + EPS) * g


def norm_proj(x, g, w, *, tl, tn, name):
    L, D = x.shape
    N = w.shape[1]

    def body(x_ref, g_ref, w_ref, o_ref, ht_ref, h_scr):
        @pl.when(pl.program_id(1) == 0)
        def _():
            h = _rms(x_ref[...], g_ref[...])
            h_scr[...] = h.astype(MXU_DTYPE)
            ht_ref[...] = h.T.astype(MXU_DTYPE)

        o_ref[...] = jnp.dot(h_scr[...], w_ref[...], preferred_element_type=F32)

    return pl.pallas_call(
        body, name=name, grid=(L // tl, N // tn),
        in_specs=[pl.BlockSpec((tl, D), lambda i, j: (i, 0)), pl.BlockSpec((1, D), lambda i, j: (0, 0)),
                  pl.BlockSpec((D, tn), lambda i, j: (0, j))],
        out_specs=[pl.BlockSpec((tl, tn), lambda i, j: (i, j)), pl.BlockSpec((D, tl), lambda i, j: (0, i))],
        out_shape=[jax.ShapeDtypeStruct((L, N), F32), jax.ShapeDtypeStruct((D, L), MXU_DTYPE)],
        scratch_shapes=[pltpu.VMEM((tl, D), MXU_DTYPE)],
        compiler_params=_params(2),
    )(x, g, w)


def norm_proj_bwd_x(dproj, x, g, w, dxn, *, tl, name):
    L, D = x.shape
    N = w.shape[1]

    def body(dp_ref, x_ref, g_ref, w_ref, dxn_ref, dx_ref, dg_ref):
        @pl.when(pl.program_id(0) == 0)
        def _():
            dg_ref[...] = jnp.zeros_like(dg_ref)

        dh = _dot_nt(dp_ref[...], w_ref[...])
        _, vjp = jax.vjp(_rms, x_ref[...], g_ref[...])
        dx, dg = vjp(dh)
        dx_ref[...] = dxn_ref[...] + dx
        dg_ref[...] += dg

    return pl.pallas_call(
        body, name=name, grid=(L // tl,),
        in_specs=[pl.BlockSpec((tl, N), lambda i: (i, 0)), pl.BlockSpec((tl, D), lambda i: (i, 0)),
                  pl.BlockSpec((1, D), lambda i: (0, 0)), WHOLE_VMEM, pl.BlockSpec((tl, D), lambda i: (i, 0))],
        out_specs=[pl.BlockSpec((tl, D), lambda i: (i, 0)), pl.BlockSpec((1, D), lambda i: (0, 0))],
        out_shape=[jax.ShapeDtypeStruct((L, D), F32), jax.ShapeDtypeStruct((1, D), F32)],
        compiler_params=_params(1),
    )(dproj, x, g, w, dxn)


def norm_proj_bwd_w(dproj, ht, *, tl, tn, name):
    D, L = ht.shape
    N = dproj.shape[1]

    def body(dp_ref, ht_ref, dw_ref):
        @pl.when(pl.program_id(1) == 0)
        def _():
            dw_ref[...] = jnp.zeros_like(dw_ref)

        dw_ref[...] += _dot(ht_ref[...], dp_ref[...])

    return pl.pallas_call(
        body, name=name, grid=(N // tn, L // tl),
        in_specs=[pl.BlockSpec((tl, tn), lambda j, i: (i, j)), pl.BlockSpec((D, tl), lambda j, i: (0, i))],
        out_specs=pl.BlockSpec((None, D, tn), lambda j, i: (j, 0, 0)),
        out_shape=jax.ShapeDtypeStruct((N // tn, D, tn), F32),
        compiler_params=_params(2),
    )(dproj, ht)


def _s5_disc(a_re, a_im, log_dt, b_re, b_im):
    dt = jnp.exp(log_dt)
    mag = jnp.exp(a_re * dt)
    abar_re = mag * jnp.cos(a_im * dt)
    abar_im = mag * jnp.sin(a_im * dt)
    nr = abar_re - 1.0
    ni = abar_im
    den = a_re * a_re + a_im * a_im
    f_re = (nr * a_re + ni * a_im) / den
    f_im = (ni * a_re - nr * a_im) / den
    bb_re = f_re[None] * b_re - f_im[None] * b_im
    bb_im = f_re[None] * b_im + f_im[None] * b_re
    return abar_re, abar_im, bb_re, bb_im


def s5_disc(a_re, a_im, log_dt, b_re, b_im, *, name):
    def body(a_re_ref, a_im_ref, dt_ref, b_re_ref, b_im_ref, *outs):
        vals = _s5_disc(a_re_ref[...], a_im_ref[...], dt_ref[...], b_re_ref[...], b_im_ref[...])
        for o_ref, val in zip(outs, vals):
            o_ref[...] = val

    sd = jax.ShapeDtypeStruct
    return pl.pallas_call(
        body, name=name,
        out_shape=[sd(a_re.shape, F32), sd(a_re.shape, F32), sd(b_re.shape, F32), sd(b_re.shape, F32)],
    )(a_re, a_im, log_dt, b_re, b_im)


def s5_disc_bwd(a_re, a_im, log_dt, b_re, b_im, d_abar_re, d_abar_im, d_bb_re, d_bb_im, *, name):
    def body(a_re_ref, a_im_ref, dt_ref, b_re_ref, b_im_ref, c0, c1, c2, c3, *outs):
        _, vjp = jax.vjp(_s5_disc, a_re_ref[...], a_im_ref[...], dt_ref[...], b_re_ref[...], b_im_ref[...])
        grads = vjp((c0[...], c1[...], c2[...], c3[...]))
        for o_ref, val in zip(outs, grads):
            o_ref[...] = val

    sd = jax.ShapeDtypeStruct
    return pl.pallas_call(
        body, name=name,
        out_shape=[sd(a_re.shape, F32), sd(a_re.shape, F32), sd(log_dt.shape, F32), sd(b_re.shape, F32), sd(b_re.shape, F32)],
    )(a_re, a_im, log_dt, b_re, b_im, d_abar_re, d_abar_im, d_bb_re, d_bb_im)


def _diag_halves(W, S):
    H, Q = W // 2, S // 2
    return [(slice(k * H, (k + 1) * H), (slice(k * Q, (k + 1) * Q), slice(S + k * Q, S + (k + 1) * Q))) for k in range(2)]


def s5_scan(proj, bb_bd, c_bd, abar_re, abar_im, d_skip, *, tl, name):
    L = proj.shape[0]
    W, S2 = bb_bd.shape
    S = S2 // 2

    def body(u_ref, bb_ref, c_ref, ar_ref, ai_ref, d_ref, y_ref, xs_ref, sr_scr, si_scr):
        @pl.when(pl.program_id(0) == 0)
        def _():
            sr_scr[...] = jnp.zeros_like(sr_scr)
            si_scr[...] = jnp.zeros_like(si_scr)

        u = u_ref[...]
        for rows, col_ranges in _diag_halves(W, S):
            for cols in col_ranges:
                xs_ref[:, cols] = _dot(u[:, rows], bb_ref[rows, cols])
        ar = ar_ref[...]
        ai = ai_ref[...]

        def step(t, carry):
            sr, si = carry
            row = pl.ds(t, 1)
            nsr = ar * sr - ai * si + xs_ref[row, pl.ds(0, S)]
            nsi = ar * si + ai * sr + xs_ref[row, pl.ds(S, S)]
            xs_ref[row, pl.ds(0, S)] = nsr
            xs_ref[row, pl.ds(S, S)] = nsi
            return nsr, nsi

        sr, si = lax.fori_loop(0, tl, step, (sr_scr[...], si_scr[...]), unroll=8)
        sr_scr[...] = sr
        si_scr[...] = si
        y = [sum(_dot(xs_ref[:, cols], c_ref[cols, rows]) for cols in col_ranges) for rows, col_ranges in _diag_halves(W, S)]
        y_ref[...] = jnp.concatenate(y, axis=1) + d_ref[...] * u

    return pl.pallas_call(
        body, name=name, grid=(L // tl,),
        in_specs=[pl.BlockSpec((tl, W), lambda i: (i, 0)), WHOLE_VMEM, WHOLE_VMEM, WHOLE_VMEM, WHOLE_VMEM, WHOLE_VMEM],
        out_specs=[pl.BlockSpec((tl, W), lambda i: (i, 0)), pl.BlockSpec((tl, S2), lambda i: (i, 0))],
        out_shape=[jax.ShapeDtypeStruct((L, W), F32), jax.ShapeDtypeStruct((L, S2), F32)],
        scratch_shapes=[pltpu.VMEM((1, S), F32), pltpu.VMEM((1, S), F32)],
        compiler_params=_params(1),
    )(proj, bb_bd, c_bd, abar_re, abar_im, d_skip)


def s5_scan_bwd(dy, proj, xs, bb_bd, c_bd, abar_re, abar_im, d_skip, *, tl, name):
    L = proj.shape[0]
    W, S2 = bb_bd.shape
    S = S2 // 2
    nt = L // tl
    HALO = 8

    def body(dy_ref, u_ref, xs_ref, xprev_ref, bb_ref, c_ref, ar_ref, ai_ref, d_ref,
             du_ref, dd_ref, dar_ref, dai_ref, dbb_hbm, dc_hbm, lam_scr, lr_scr, li_scr, dbb_scr, dc_scr, sem):
        i = pl.program_id(0)

        @pl.when(i == 0)
        def _():
            for ref in (lr_scr, li_scr, dbb_scr, dc_scr, dd_ref, dar_ref, dai_ref):
                ref[...] = jnp.zeros_like(ref)

        dy = dy_ref[...]
        u = u_ref[...]
        xs_v = xs_ref[...]
        halves = _diag_halves(W, S)
        for rows, col_ranges in halves:
            for cols in col_ranges:
                dc_scr[cols, rows] += _dot_tn(xs_ref[:, cols], dy[:, rows])
                lam_scr[:, cols] = _dot_nt(dy[:, rows], c_ref[cols, rows])
        ar = ar_ref[...]
        ai = ai_ref[...]

        def step(s, carry):
            lr, li = carry
            row = pl.ds(tl - 1 - s, 1)
            nlr = lam_scr[row, pl.ds(0, S)] + ar * lr + ai * li
            nli = lam_scr[row, pl.ds(S, S)] - ai * lr + ar * li
            lam_scr[row, pl.ds(0, S)] = nlr
            lam_scr[row, pl.ds(S, S)] = nli
            return nlr, nli

        lr, li = lax.fori_loop(0, tl, step, (lr_scr[...], li_scr[...]), unroll=8)
        lr_scr[...] = lr
        li_scr[...] = li
        lam = lam_scr[...]
        before_tile = jnp.where(i == nt - 1, 0.0, xprev_ref[pl.ds(HALO - 1, 1), :])
        first_row = lax.broadcasted_iota(jnp.int32, (tl, 1), 0) == 0
        x_prev = jnp.where(first_row, before_tile, pltpu.roll(xs_v, 1, 0))
        lam_r, lam_i = lam[:, :S], lam[:, S:]
        xp_r, xp_i = x_prev[:, :S], x_prev[:, S:]
        dar_ref[...] += jnp.sum(lam_r * xp_r + lam_i * xp_i, axis=0, keepdims=True)
        dai_ref[...] += jnp.sum(lam_i * xp_r - lam_r * xp_i, axis=0, keepdims=True)
        du = [sum(_dot_nt(lam_scr[:, cols], bb_ref[rows, cols]) for cols in col_ranges) for rows, col_ranges in halves]
        du_ref[...] = dy * d_ref[...] + jnp.concatenate(du, axis=1)
        for rows, col_ranges in halves:
            for cols in col_ranges:
                dbb_scr[rows, cols] += _dot_tn(u[:, rows], lam_scr[:, cols])
        dd_ref[...] += jnp.sum(dy * u, axis=0, keepdims=True)

        @pl.when(i == nt - 1)
        def _():
            for src, dst in ((dbb_scr, dbb_hbm), (dc_scr, dc_hbm)):
                cp = pltpu.make_async_copy(src, dst, sem)
                cp.start()
                cp.wait()

    rev = lambda i: (nt - 1 - i, 0)
    const = lambda i: (0, 0)
    halo = lambda i: (jnp.maximum((nt - 1 - i) * (tl // HALO) - 1, 0), 0)
    sd = jax.ShapeDtypeStruct
    return pl.pallas_call(
        body, name=name, grid=(nt,),
        in_specs=[pl.BlockSpec((tl, W), rev), pl.BlockSpec((tl, W), rev), pl.BlockSpec((tl, S2), rev),
                  pl.BlockSpec((HALO, S2), halo), WHOLE_VMEM, WHOLE_VMEM, WHOLE_VMEM, WHOLE_VMEM, WHOLE_VMEM],
        out_specs=[pl.BlockSpec((tl, W), rev), pl.BlockSpec((1, W), const), pl.BlockSpec((1, S), const),
                   pl.BlockSpec((1, S), const), ANY, ANY],
        out_shape=[sd((L, W), F32), sd((1, W), F32), sd((1, S), F32), sd((1, S), F32), sd((W, S2), F32), sd((S2, W), F32)],
        scratch_shapes=[pltpu.VMEM((tl, S2), F32), pltpu.VMEM((1, S), F32), pltpu.VMEM((1, S), F32),
                        pltpu.VMEM((W, S2), F32), pltpu.VMEM((S2, W), F32), pltpu.SemaphoreType.DMA],
        compiler_params=_params(1),
    )(dy, proj, xs, xs, bb_bd, c_bd, abar_re, abar_im, d_skip)


def _glu_gate(gl, z):
    half = gl.shape[1] // 2
    return gl[:, :half] * jax.nn.sigmoid(gl[:, half:]) * jax.nn.silu(z)


def glu_gate(y, proj, w, b, *, tl, name):
    L, W = y.shape

    def body(y_ref, z_ref, w_ref, b_ref, o_ref):
        gl = _dot(jax.nn.gelu(y_ref[...]), w_ref[...]) + b_ref[...]
        o_ref[...] = _glu_gate(gl, z_ref[...])

    return pl.pallas_call(
        body, name=name, grid=(L // tl,),
        in_specs=[pl.BlockSpec((tl, W), lambda i: (i, 0)), pl.BlockSpec((tl, W), lambda i: (i, 1)), WHOLE_VMEM, WHOLE_VMEM],
        out_specs=pl.BlockSpec((tl, W), lambda i: (i, 0)),
        out_shape=jax.ShapeDtypeStruct((L, W), F32),
        compiler_params=_params(1),
    )(y, proj, w, b)


def glu_gate_bwd(dys, y, proj, w, b, *, tl, name):
    L, W = y.shape

    def body(dys_ref, y_ref, z_ref, w_ref, b_ref, dy_ref, dz_ref, dw_ref, db_ref):
        @pl.when(pl.program_id(0) == 0)
        def _():
            dw_ref[...] = jnp.zeros_like(dw_ref)
            db_ref[...] = jnp.zeros_like(db_ref)

        yg, gelu_vjp = jax.vjp(jax.nn.gelu, y_ref[...])
        gl = _dot(yg, w_ref[...]) + b_ref[...]
        _, gate_vjp = jax.vjp(_glu_gate, gl, z_ref[...])
        dgl, dz = gate_vjp(dys_ref[...])
        dz_ref[...] = dz
        dy_ref[...] = gelu_vjp(_dot_nt(dgl, w_ref[...]))[0]
        dw_ref[...] += _dot_tn(yg, dgl)
        db_ref[...] += jnp.sum(dgl, axis=0, keepdims=True)

    row = lambda i: (i, 0)
    const = lambda i: (0, 0)
    sd = jax.ShapeDtypeStruct
    return pl.pallas_call(
        body, name=name, grid=(L // tl,),
        in_specs=[pl.BlockSpec((tl, W), row), pl.BlockSpec((tl, W), row), pl.BlockSpec((tl, W), lambda i: (i, 1)),
                  WHOLE_VMEM, WHOLE_VMEM],
        out_specs=[pl.BlockSpec((tl, W), row), pl.BlockSpec((tl, W), row), pl.BlockSpec((W, 2 * W), const),
                   pl.BlockSpec((1, 2 * W), const)],
        out_shape=[sd((L, W), F32), sd((L, W), F32), sd((W, 2 * W), F32), sd((1, 2 * W), F32)],
        compiler_params=_params(1),
    )(dys, y, proj, w, b)


def _logs(z, mask):
    t = jnp.minimum(z, 0.0)
    log_beta = t - jnp.log(1.0 + jnp.exp(t + t - z))
    log_1m = log_beta - z
    if mask is not None:
        log_1m = jnp.where(mask, log_1m, 0.0)
    hi = log_1m.astype(MXU_DTYPE)
    lo = (log_1m - hi.astype(F32)).astype(MXU_DTYPE)
    return log_beta, log_1m, jnp.concatenate([hi, lo], axis=1)


def _keys(t, tq):
    return pl.ds(pl.multiple_of(t * tq, tq), tq)


def _span(j, nk):
    return [j * nk + h for h in reversed(range(nk))]


def _prefetch(x_i, t_ref, j, nk, tq, scr):
    for h, t in enumerate(_span(jnp.maximum(j, 0), nk)):
        scr[h] = _dot(x_i, t_ref[:, _keys(t, tq)])


def columns_t(a, blocks, scales, *, width, tl, name):
    L = a.shape[0]
    n = len(blocks)

    def body(*refs):
        for a_ref, o_ref, s in zip(refs[:n], refs[n:], scales):
            o_ref[...] = (a_ref[...] * s).T.astype(MXU_DTYPE)

    return pl.pallas_call(
        body, name=name, grid=(L // tl,),
        in_specs=[pl.BlockSpec((tl, width), functools.partial(lambda b, i: (i, b), b)) for b in blocks],
        out_specs=[pl.BlockSpec((width, tl), lambda i: (0, i))] * n,
        out_shape=[jax.ShapeDtypeStruct((width, L), MXU_DTYPE)] * n,
        compiler_params=_params(1),
    )(*[a] * n)


def rows_from_t(arrays_t, *, tl, name):
    W, L = arrays_t[0].shape
    n = len(arrays_t)

    def body(*refs):
        for a_ref, o_ref in zip(refs[:n], refs[n:]):
            o_ref[...] = a_ref[...].T

    return pl.pallas_call(
        body, name=name, grid=(L // tl,),
        in_specs=[pl.BlockSpec((W, tl), lambda i: (0, i))] * n,
        out_specs=[pl.BlockSpec((tl, W), lambda i: (i, 0))] * n,
        out_shape=[jax.ShapeDtypeStruct((L, W), F32)] * n,
        compiler_params=_params(1),
    )(*arrays_t)


def _weights(z, masks, R, tri2, between):
    logs = [_logs(zv, m) for zv, m in zip(z, masks)]
    after = [jnp.dot(split, tri2, preferred_element_type=F32) for _, _, split in logs]
    between()
    w = []
    for (log_beta, log_1m, _), a, m in zip(logs, after, masks):
        wt = jnp.exp(log_beta + a + R)
        w.append(wt if m is None else jnp.where(m, wt, 0.0))
        R = R + jnp.sum(log_1m, axis=1, keepdims=True)
    return [lg[0] for lg in logs], w, R


def _iotas(t):
    return lax.broadcasted_iota(jnp.int32, (t, t), 0), lax.broadcasted_iota(jnp.int32, (t, t), 1)


def _suffix_tri2(row, col):
    tri = (row > col).astype(MXU_DTYPE)
    return jnp.concatenate([tri, tri], axis=0)


def _rows(t_ref):
    return t_ref[...].astype(F32).T.astype(MXU_DTYPE)


def attn_fwd(qt, kt, vt, *, tq, nk, name):
    nh, hd, L = qt.shape
    assert L % (tq * nk) == 0

    def body(qt_ref, kt_ref, vt_ref, o_ref, z_scr, w_scr):
        i = pl.program_id(1)
        q_i = _rows(qt_ref)
        row, col = _iotas(tq)
        tri2 = _suffix_tri2(row, col)
        causal = col < row
        J = i // nk

        def weights(j, R):
            z = [z_scr[h] for h in range(nk)]
            _, w, R = _weights(z, [None] * nk, R, tri2, lambda: _prefetch(q_i, kt_ref, j - 1, nk, tq, z_scr))
            for h, wt in enumerate(w):
                w_scr[h] = wt.astype(w_scr.dtype)
            return R

        def diagonal(n):
            def run(R):
                z = [_dot(q_i, kt_ref[:, _keys(i - h, tq)]) for h in range(n)]
                _, w, R = _weights(z, [causal] + [None] * (n - 1), R, tri2, lambda: _prefetch(q_i, kt_ref, J - 1, nk, tq, z_scr))
                for h in range(nk - n):
                    w_scr[h] = jnp.zeros((tq, tq), w_scr.dtype)
                for h, wt in enumerate(w):
                    w_scr[nk - n + h] = wt.astype(w_scr.dtype)
                return R
            return run

        def flush(j, acc):
            for h, t in enumerate(_span(j, nk)):
                acc = acc + _dot_nt(vt_ref[:, _keys(t, tq)], w_scr[h])
            return acc

        R = lax.switch(i % nk, [diagonal(n + 1) for n in range(nk)], jnp.zeros((tq, 1), F32))

        def step(s, carry):
            acc, R = carry
            j = J - 1 - s
            acc = flush(j + 1, acc)
            return acc, weights(j, R)

        acc, _ = lax.fori_loop(0, J, step, (jnp.zeros((hd, tq), F32), R))
        o_ref[...] = flush(0, acc)

    tile = pl.BlockSpec((None, hd, tq), lambda h, i: (h, 0, i))
    whole = pl.BlockSpec((None, hd, L), lambda h, i: (h, 0, 0))
    return pl.pallas_call(
        body, name=name, grid=(nh, L // tq),
        in_specs=[tile, whole, whole], out_specs=tile,
        out_shape=jax.ShapeDtypeStruct((nh, hd, L), F32),
        scratch_shapes=[pltpu.VMEM((nk, tq, tq), F32), pltpu.VMEM((nk, tq, tq), MXU_DTYPE)],
        compiler_params=_params(2),
    )(qt, kt, vt)


def attn_bwd(qt, kt, vt, dot, *, tq, nk, name):
    nh, hd, L = qt.shape
    assert L % (tq * nk) == 0
    nq = L // tq
    scale = hd ** -0.5

    def body(qt_ref, kt_ref, vt_ref, dot_ref, dq_ref, dk_ref, dv_ref, e_scr, beta_scr, z_scr, dw_scr, w_scr):
        i = pl.program_id(1)

        @pl.when(i == 0)
        def _():
            dk_ref[...] = jnp.zeros_like(dk_ref)
            dv_ref[...] = jnp.zeros_like(dv_ref)

        q_i = _rows(qt_ref)
        do_i = _rows(dot_ref)
        dot_i = dot_ref[...]
        qt_i = qt_ref[...]
        row, col = _iotas(tq)
        tri2 = _suffix_tri2(row, col)
        prefix_tri = (row < col).astype(MXU_DTYPE)
        causal = col < row
        J = i // nk

        def prefetch(j):
            _prefetch(q_i, kt_ref, j, nk, tq, z_scr)
            _prefetch(do_i, vt_ref, j, nk, tq, dw_scr)

        def keep(slot, t, log_beta, w, dw, mask):
            w_scr[slot] = w.astype(w_scr.dtype)
            e_scr[t] = dw * w
            beta = jnp.exp(log_beta)
            beta_scr[t] = (beta if mask is None else jnp.where(mask, beta, 0.0)).astype(beta_scr.dtype)

        def rebuild(j, R):
            dw = [dw_scr[h] for h in range(nk)]
            log_beta, w, R = _weights([z_scr[h] for h in range(nk)], [None] * nk, R, tri2, lambda: prefetch(j - 1))
            for h, t in enumerate(_span(j, nk)):
                keep(h, t, log_beta[h], w[h], dw[h], None)
            return R

        def diagonal(n):
            def run(R):
                z = [_dot(q_i, kt_ref[:, _keys(i - h, tq)]) for h in range(n)]
                dw = [_dot(do_i, vt_ref[:, _keys(i - h, tq)]) for h in range(n)]
                masks = [causal] + [None] * (n - 1)
                log_beta, w, R = _weights(z, masks, R, tri2, lambda: prefetch(J - 1))
                for h in range(nk - n):
                    t = i + nk - n - h
                    w_scr[h] = jnp.zeros((tq, tq), w_scr.dtype)
                    e_scr[t] = jnp.zeros((tq, tq), F32)
                    beta_scr[t] = jnp.zeros((tq, tq), beta_scr.dtype)
                for h in range(n):
                    keep(nk - n + h, i - h, log_beta[h], w[h], dw[h], masks[h])
                return R
            return run

        def flush(j):
            for h, t in enumerate(_span(j, nk)):
                dv_ref[:, _keys(t, tq)] += _dot(dot_i, w_scr[h])

        R = lax.switch(i % nk, [diagonal(n + 1) for n in range(nk)], jnp.zeros((tq, 1), F32))

        def step(s, R):
            j = J - 1 - s
            flush(j + 1)
            return rebuild(j, R)

        lax.fori_loop(0, J, step, R)
        flush(0)

        def push(j, carry):
            dq, P = carry
            tiles = [j * nk + h for h in range(nk)]
            e = [e_scr[t] for t in tiles]
            before = [_dot(ev, prefix_tri) for ev in e]
            for t, ev, bv in zip(tiles, e, before):
                dz = (ev - beta_scr[t].astype(F32) * (ev + bv + P)).astype(MXU_DTYPE)
                dk_ref[:, _keys(t, tq)] += _dot(qt_i, dz)
                dq = dq + _dot_nt(kt_ref[:, _keys(t, tq)], dz)
                P = P + jnp.sum(ev, axis=1, keepdims=True)
            return dq, P

        dq, _ = lax.fori_loop(0, J + 1, push, (jnp.zeros((hd, tq), F32), jnp.zeros((tq, 1), F32)))
        dq_ref[...] = dq * scale

    tile = pl.BlockSpec((None, hd, tq), lambda h, i: (h, 0, i))
    whole = pl.BlockSpec((None, hd, L), lambda h, i: (h, 0, 0))
    sd = jax.ShapeDtypeStruct((nh, hd, L), F32)
    return pl.pallas_call(
        body, name=name, grid=(nh, nq),
        in_specs=[tile, whole, whole, tile], out_specs=[tile, whole, whole],
        out_shape=[sd, sd, sd],
        scratch_shapes=[pltpu.VMEM((nq, tq, tq), F32), pltpu.VMEM((nq, tq, tq), MXU_DTYPE), pltpu.VMEM((nk, tq, tq), F32),
                        pltpu.VMEM((nk, tq, tq), F32), pltpu.VMEM((nk, tq, tq), MXU_DTYPE)],
        compiler_params=_params(2),
    )(qt, kt, vt, dot)


def _block_diag(blocks):
    G, m, n = blocks.shape
    eye = jnp.eye(G, dtype=blocks.dtype)
    return (blocks[:, :, None, :] * eye[:, None, :, None]).reshape(G * m, G * n)


def _diag_blocks(dense, m, n):
    G = dense.shape[0] // m
    return jnp.einsum("gmgn->gmn", dense.reshape(G, m, G, n))


def s5_prepare(a_re, a_im, log_dt, b_re, b_im, c_re, c_im, d_skip, *, name):
    G, P = a_re.shape
    b_re_t = b_re.transpose(2, 0, 1)
    b_im_t = b_im.transpose(2, 0, 1)
    log_dt = log_dt.reshape(G, 1)
    abar_re, abar_im, bb_re, bb_im = s5_disc(a_re, a_im, log_dt, b_re_t, b_im_t, name=name)
    bb_bd = jnp.concatenate([_block_diag(bb_re.transpose(1, 0, 2)), _block_diag(bb_im.transpose(1, 0, 2))], axis=1)
    c_bd = jnp.concatenate([_block_diag(c_re.transpose(0, 2, 1)), -_block_diag(c_im.transpose(0, 2, 1))], axis=0)
    return dict(a_re=a_re, a_im=a_im, log_dt=log_dt, b_re_t=b_re_t, b_im_t=b_im_t,
                abar_re=abar_re.reshape(1, G * P), abar_im=abar_im.reshape(1, G * P),
                bb_bd=bb_bd.astype(MXU_DTYPE), c_bd=c_bd.astype(MXU_DTYPE), d=d_skip.reshape(1, -1))


def s5_param_grads(prm, dd, dar, dai, dbb_bd, dc_bd, *, name):
    G, P = prm["a_re"].shape
    H = prm["b_re_t"].shape[0]
    S = G * P
    d_bb_re = _diag_blocks(dbb_bd[:, :S], H, P).transpose(1, 0, 2)
    d_bb_im = _diag_blocks(dbb_bd[:, S:], H, P).transpose(1, 0, 2)
    da_re, da_im, dlog_dt, db_re_t, db_im_t = s5_disc_bwd(
        prm["a_re"], prm["a_im"], prm["log_dt"], prm["b_re_t"], prm["b_im_t"],
        dar.reshape(G, P), dai.reshape(G, P), d_bb_re, d_bb_im, name=name)
    return dict(a_re=da_re, a_im=da_im, log_dt=dlog_dt.reshape(G), b_re=db_re_t.transpose(1, 2, 0), b_im=db_im_t.transpose(1, 2, 0),
                c_re=_diag_blocks(dc_bd[:S], P, H).transpose(0, 2, 1), c_im=-_diag_blocks(dc_bd[S:], P, H).transpose(0, 2, 1),
                d=dd.reshape(-1))


def _silu_gate(o, z):
    return o * jax.nn.silu(z)


def _merge(g_ssm, p_ssm, g_attn, p_attn):
    return jax.nn.sigmoid(g_ssm) * p_ssm + jax.nn.sigmoid(g_attn) * p_attn


def _proj_gates(tl, W, D):
    return [pl.BlockSpec((tl, W), lambda i: (i, 5)), pl.BlockSpec((tl, D), lambda i: (i, 3)), pl.BlockSpec((tl, D), lambda i: (i, 4))]


def merge_out(ys, o, proj, x, w_bs, w_ba, w_out, g_post, *, tl, name):
    L, W = ys.shape
    D = x.shape[1]

    def body(ys_ref, o_ref, z_ref, gs_ref, ga_ref, x_ref, wbs_ref, wba_ref, wout_ref, g_ref, xn_ref, out_ref):
        ya = _silu_gate(o_ref[...], z_ref[...])
        merged = _merge(gs_ref[...], _dot(ys_ref[...], wbs_ref[...]), ga_ref[...], _dot(ya, wba_ref[...]))
        out = _dot(merged, wout_ref[...])
        out_ref[...] = out
        xn_ref[...] = x_ref[...] + _rms(out, g_ref[...])

    row = lambda i: (i, 0)
    sd = jax.ShapeDtypeStruct
    return pl.pallas_call(
        body, name=name, grid=(L // tl,),
        in_specs=[pl.BlockSpec((tl, W), row), pl.BlockSpec((tl, W), row), *_proj_gates(tl, W, D), pl.BlockSpec((tl, D), row),
                  WHOLE_VMEM, WHOLE_VMEM, WHOLE_VMEM, WHOLE_VMEM],
        out_specs=[pl.BlockSpec((tl, D), row), pl.BlockSpec((tl, D), row)],
        out_shape=[sd((L, D), F32), sd((L, D), F32)],
        compiler_params=_params(1),
    )(ys, o, proj, proj, proj, x, w_bs, w_ba, w_out, g_post)


def merge_out_bwd(dxn, out, ys, o, proj, w_bs, w_ba, w_out, g_post, *, tl, name):
    L, W = ys.shape
    D = dxn.shape[1]

    def body(dxn_ref, out_ref, ys_ref, o_ref, z_ref, gs_ref, ga_ref, wbs_ref, wba_ref, wout_ref, g_ref,
             dys_ref, do_ref, dtail_ref, dwout_ref, dwbs_ref, dwba_ref, dg_ref):
        @pl.when(pl.program_id(0) == 0)
        def _():
            for ref in (dwout_ref, dwbs_ref, dwba_ref, dg_ref):
                ref[...] = jnp.zeros_like(ref)

        _, rms_vjp = jax.vjp(_rms, out_ref[...], g_ref[...])
        dout, dg = rms_vjp(dxn_ref[...])
        dg_ref[...] += dg
        ys_v = ys_ref[...]
        ya, gate_vjp = jax.vjp(_silu_gate, o_ref[...], z_ref[...])
        merged, merge_vjp = jax.vjp(_merge, gs_ref[...], _dot(ys_v, wbs_ref[...]), ga_ref[...], _dot(ya, wba_ref[...]))
        dwout_ref[...] += _dot_tn(merged, dout)
        dgs, dps, dga, dpa = merge_vjp(_dot_nt(dout, wout_ref[...]))
        dys_ref[...] = _dot_nt(dps, wbs_ref[...])
        dwbs_ref[...] += _dot_tn(ys_v, dps)
        dwba_ref[...] += _dot_tn(ya, dpa)
        do, dz = gate_vjp(_dot_nt(dpa, wba_ref[...]))
        do_ref[...] = do
        dtail_ref[:, pl.ds(0, W)] = dz
        dtail_ref[:, pl.ds(W, D)] = dgs
        dtail_ref[:, pl.ds(W + D, D)] = dga

    row = lambda i: (i, 0)
    const = lambda i: (0, 0)
    sd = jax.ShapeDtypeStruct
    return pl.pallas_call(
        body, name=name, grid=(L // tl,),
        in_specs=[pl.BlockSpec((tl, D), row), pl.BlockSpec((tl, D), row), pl.BlockSpec((tl, W), row), pl.BlockSpec((tl, W), row),
                  *_proj_gates(tl, W, D), WHOLE_VMEM, WHOLE_VMEM, WHOLE_VMEM, WHOLE_VMEM],
        out_specs=[pl.BlockSpec((tl, W), row), pl.BlockSpec((tl, W), row), pl.BlockSpec((tl, W + 2 * D), row),
                   pl.BlockSpec((D, D), const), pl.BlockSpec((W, D), const), pl.BlockSpec((W, D), const), pl.BlockSpec((1, D), const)],
        out_shape=[sd((L, W), F32), sd((L, W), F32), sd((L, W + 2 * D), F32), sd((D, D), F32), sd((W, D), F32), sd((W, D), F32),
                   sd((1, D), F32)],
        compiler_params=_params(1),
    )(dxn, out, ys, o, proj, proj, proj, w_bs, w_ba, w_out, g_post)


def loss_grad(y, target, *, tl, name):
    L, D = y.shape

    def body(y_ref, t_ref, dy_ref, loss_ref):
        @pl.when(pl.program_id(0) == 0)
        def _():
            loss_ref[...] = jnp.zeros_like(loss_ref)

        err = y_ref[...] - t_ref[...]
        dy_ref[...] = err / D
        loss_ref[...] += 0.5 * jnp.sum(jnp.mean(err * err, axis=-1, keepdims=True), axis=0, keepdims=True)

    row = lambda i: (i, 0)
    return pl.pallas_call(
        body, name=name, grid=(L // tl,),
        in_specs=[pl.BlockSpec((tl, D), row), pl.BlockSpec((tl, D), row)],
        out_specs=[pl.BlockSpec((tl, D), row), pl.BlockSpec((1, 1), lambda i: (0, 0))],
        out_shape=[jax.ShapeDtypeStruct((L, D), F32), jax.ShapeDtypeStruct((1, 1), F32)],
        compiler_params=_params(1),
    )(y, target)


def adamw_from_parts(parts, w, m, v, *, tr, name):
    n, R, C = parts.shape

    def body(p_ref, w_ref, m_ref, v_ref, g_ref, d_ref, nm_ref, nv_ref):
        g = p_ref[0].astype(F32)
        for s in range(1, n):
            g = g + p_ref[s].astype(F32)
        nm = ADAM_B1 * m_ref[...] + (1.0 - ADAM_B1) * g
        nv = ADAM_B2 * v_ref[...] + (1.0 - ADAM_B2) * (g * g)
        m_hat = nm / (1.0 - ADAM_B1 ** ADAM_STEP)
        v_hat = nv / (1.0 - ADAM_B2 ** ADAM_STEP)
        g_ref[...] = g
        d_ref[...] = -ADAM_LR * (m_hat / (jnp.sqrt(v_hat) + ADAM_EPS) + ADAM_WD * w_ref[...])
        nm_ref[...] = nm
        nv_ref[...] = nv

    tile = pl.BlockSpec((tr, C), lambda i: (i, 0))
    sd = jax.ShapeDtypeStruct((R, C), F32)
    return pl.pallas_call(
        body, name=name, grid=(R // tr,),
        in_specs=[pl.BlockSpec((n, tr, C), lambda i: (0, i, 0)), tile, tile, tile],
        out_specs=[tile, tile, tile, tile], out_shape=[sd, sd, sd, sd],
        compiler_params=_params(1),
    )(parts, w, m, v)


PER_PEER = N_DEV - 1


def _position():
    return lax.axis_index("x"), lax.axis_index("y"), lax.axis_index("c")


def _flat(x, y, c):
    return 4 * x + 2 * y + c


def all_gather(shards, *, name):
    n = len(shards)

    def body(*refs):
        ins, outs = refs[:n], refs[n:2 * n]
        send_sems, recv_sems, local_sems = refs[2 * n:]
        x, y, c = _position()
        me, sibling = (x, y, c), (x, y, 1 - c)
        chips = [(1 - x, y), (x, 1 - y), (1 - x, 1 - y)]

        def slot(a, dev):
            return outs[a].at[_flat(*dev)]

        def copy(a, k, block, to, src=None):
            return pltpu.make_async_remote_copy(
                src_ref=slot(a, block) if src is None else src, dst_ref=slot(a, block),
                send_sem=send_sems.at[a * PER_PEER + k], recv_sem=recv_sems.at[a * PER_PEER + k],
                device_id=to, device_id_type=MESH)

        mine = [pltpu.make_async_copy(ins[a], slot(a, me), local_sems.at[a]) for a in range(n)]
        for cp in mine:
            cp.start()
        first = []
        for a in range(n):
            first.append(copy(a, 0, me, sibling, src=ins[a]))
            first += [copy(a, 1 + j, me, (*chip, c), src=ins[a]) for j, chip in enumerate(chips)]
        for cp in first:
            cp.start()
        passed = []
        for j, chip in enumerate(chips):
            for a in range(n):
                copy(a, 1 + j, (*chip, c), me).wait_recv()
                passed.append(copy(a, 4 + j, (*chip, c), sibling))
                passed[-1].start()
        for a in range(n):
            copy(a, 0, sibling, me).wait_recv()
            for j, chip in enumerate(chips):
                copy(a, 4 + j, (*chip, 1 - c), me).wait_recv()
        for cp in first + passed:
            cp.wait_send()
        for cp in mine:
            cp.wait()

    return pl.pallas_call(
        body, name=name,
        in_specs=[ANY] * n, out_specs=[ANY] * n,
        out_shape=[jax.ShapeDtypeStruct((N_DEV, *s.shape), s.dtype) for s in shards],
        scratch_shapes=[pltpu.SemaphoreType.DMA((n * PER_PEER,)), pltpu.SemaphoreType.DMA((n * PER_PEER,)),
                        pltpu.SemaphoreType.DMA((n,))],
    )(*shards)


def exchange_blocks(parts, *, name):
    n = len(parts)

    def body(*refs):
        ins, outs = refs[:n], refs[n:2 * n]
        send_sems, recv_sems, local_sems = refs[2 * n:]
        x, y, c = _position()
        me = _flat(x, y, c)
        peers = []
        for k in range(1, N_DEV):
            peer = (1 - x if k & 4 else x, 1 - y if k & 2 else y, 1 - c if k & 1 else c)
            peers.append((k - 1, peer, _flat(*peer)))

        mine = [pltpu.make_async_copy(ins[a].at[me], outs[a].at[me], local_sems.at[a]) for a in range(n)]
        for cp in mine:
            cp.start()
        sends = [pltpu.make_async_remote_copy(
            src_ref=ins[a].at[peer_flat], dst_ref=outs[a].at[me],
            send_sem=send_sems.at[a * PER_PEER + k], recv_sem=recv_sems.at[a * PER_PEER + k],
            device_id=peer, device_id_type=MESH) for a in range(n) for k, peer, peer_flat in peers]
        for cp in sends:
            cp.start()
        for a in range(n):
            for k, peer, peer_flat in peers:
                pltpu.make_async_remote_copy(
                    src_ref=ins[a].at[me], dst_ref=outs[a].at[peer_flat],
                    send_sem=send_sems.at[a * PER_PEER + k], recv_sem=recv_sems.at[a * PER_PEER + k],
                    device_id=peer, device_id_type=MESH).wait_recv()
        for cp in sends:
            cp.wait_send()
        for cp in mine:
            cp.wait()

    return pl.pallas_call(
        body, name=name,
        in_specs=[ANY] * n, out_specs=[ANY] * n,
        out_shape=[jax.ShapeDtypeStruct(p.shape, p.dtype) for p in parts],
        scratch_shapes=[pltpu.SemaphoreType.DMA((n * PER_PEER,)), pltpu.SemaphoreType.DMA((n * PER_PEER,)),
                        pltpu.SemaphoreType.DMA((n,))],
    )(*parts)


LANES = 128
SMALL = ("pre_norm_g", "post_norm_g", "ssm_a_re", "ssm_a_im", "ssm_log_dt", "ssm_b_re", "ssm_b_im", "ssm_c_re", "ssm_c_im",
         "ssm_d", "b_glu")
COLUMN_SHARDED = ("w_in", "w_glu", "w_branch_ssm", "w_branch_attn")
WEIGHTS = ("pre_norm_g", "post_norm_g", "w_in", "ssm_a_re", "ssm_a_im", "ssm_log_dt", "ssm_b_re", "ssm_b_im", "ssm_c_re",
           "ssm_c_im", "ssm_d", "w_glu", "b_glu", "w_branch_ssm", "w_branch_attn", "w_out")
PACK_ROWS = 256


def _pack(arrays):
    flat = jnp.concatenate([a.reshape(-1) for a in arrays])
    tile = PACK_ROWS * LANES
    return jnp.pad(flat, (0, -flat.size % tile)).reshape(-1, LANES)


def _unpack(packed, like):
    flat, out, at = packed.reshape(-1), [], 0
    for a in like:
        out.append(flat[at:at + a.size].reshape(a.shape))
        at += a.size
    return out


def _row_tile(rows):
    return next(t for t in (256, 128, 64, 32, 16, 8) if rows % t == 0)


def kernel(x, pre_norm_g, post_norm_g, w_in, ssm_a_re, ssm_a_im, ssm_log_dt, ssm_b_re, ssm_b_im, ssm_c_re, ssm_c_im, ssm_d, w_glu, b_glu, w_branch_ssm, w_branch_attn, w_out, loss_target, m_pre_norm_g, m_post_norm_g, m_w_in, m_ssm_a_re, m_ssm_a_im, m_ssm_log_dt, m_ssm_b_re, m_ssm_b_im, m_ssm_c_re, m_ssm_c_im, m_ssm_d, m_w_glu, m_b_glu, m_w_branch_ssm, m_w_branch_attn, m_w_out, v_pre_norm_g, v_post_norm_g, v_w_in, v_ssm_a_re, v_ssm_a_im, v_ssm_log_dt, v_ssm_b_re, v_ssm_b_im, v_ssm_c_re, v_ssm_c_im, v_ssm_d, v_w_glu, v_b_glu, v_w_branch_ssm, v_w_branch_attn, v_w_out):
    given = dict(locals())
    weights = {n: given[n] for n in WEIGHTS}
    depth, D = pre_norm_g.shape
    h = x[0]
    L = h.shape[0]
    W = w_glu.shape[1]
    by_head = (N_HEADS, HEAD_DIM, L)

    gathered = all_gather([weights[n].astype(MXU_DTYPE) for n in (*COLUMN_SHARDED, "w_out")], name="gather_weights")
    full = {n: g.transpose(1, 2, 0, 3).reshape(depth, g.shape[2], -1) for n, g in zip(COLUMN_SHARDED, gathered)}
    full["w_out"] = gathered[-1].transpose(1, 0, 2, 3).reshape(depth, -1, D)

    saved = []
    for l in range(depth):
        prm = s5_prepare(ssm_a_re[l], ssm_a_im[l], ssm_log_dt[l], ssm_b_re[l], ssm_b_im[l], ssm_c_re[l], ssm_c_im[l], ssm_d[l],
                         name=f"s5_disc_{l}")
        proj, ht = norm_proj(h, pre_norm_g[l][None], full["w_in"][l], tl=512, tn=1280, name=f"norm_proj_{l}")
        y, states = s5_scan(proj, prm["bb_bd"], prm["c_bd"], prm["abar_re"], prm["abar_im"], prm["d"], tl=256, name=f"s5_scan_{l}")
        ys = glu_gate(y, proj, full["w_glu"][l], b_glu[l][None], tl=512, name=f"glu_gate_{l}")
        qt, kt, vt = (t.reshape(by_head) for t in columns_t(proj, (2, 3, 4), (HEAD_DIM ** -0.5, 1.0, 1.0), width=W, tl=512,
                                                            name=f"qkv_t_{l}"))
        (o,) = rows_from_t([attn_fwd(qt, kt, vt, tq=ATTN_TILE, nk=ATTN_SPAN, name=f"attn_fwd_{l}").reshape(W, L)], tl=512,
                           name=f"attn_out_{l}")
        h_next, out = merge_out(ys, o, proj, h, full["w_branch_ssm"][l], full["w_branch_attn"][l], full["w_out"][l],
                                post_norm_g[l][None], tl=256, name=f"merge_out_{l}")
        saved.append(dict(x=h, ht=ht, prm=prm, proj=proj, y=y, states=states, ys=ys, qt=qt, kt=kt, vt=vt, o=o, out=out))
        h = h_next

    dx, loss_part = loss_grad(h, loss_target[0], tl=512, name="loss_grad")
    loss = lax.psum(loss_part[0, 0], MESH_AXES)

    grads = {n: [None] * depth for n in WEIGHTS}
    for l in reversed(range(depth)):
        s = saved[l]
        prm = s["prm"]
        dys, do, dtail, grads["w_out"][l], grads["w_branch_ssm"][l], grads["w_branch_attn"][l], dg_post = merge_out_bwd(
            dx, s["out"], s["ys"], s["o"], s["proj"], full["w_branch_ssm"][l], full["w_branch_attn"][l], full["w_out"][l],
            post_norm_g[l][None], tl=256, name=f"merge_out_bwd_{l}")
        (do_t,) = columns_t(do, (0,), (1.0,), width=W, tl=512, name=f"do_t_{l}")
        dqkv_t = attn_bwd(s["qt"], s["kt"], s["vt"], do_t.reshape(by_head), tq=ATTN_TILE, nk=ATTN_SPAN, name=f"attn_bwd_{l}")
        dq, dk, dv = rows_from_t([t.reshape(W, L) for t in dqkv_t], tl=512, name=f"attn_grads_{l}")
        dy, dz_ssm, grads["w_glu"][l], db_glu = glu_gate_bwd(dys, s["y"], s["proj"], full["w_glu"][l], b_glu[l][None], tl=512,
                                                             name=f"glu_gate_bwd_{l}")
        du, *s5_outs = s5_scan_bwd(dy, s["proj"], s["states"], prm["bb_bd"], prm["c_bd"], prm["abar_re"], prm["abar_im"], prm["d"],
                                   tl=256, name=f"s5_scan_bwd_{l}")
        for n, g in s5_param_grads(prm, *s5_outs, name=f"s5_disc_bwd_{l}").items():
            grads["ssm_" + n][l] = g
        dproj = jnp.concatenate([du, dz_ssm, dq, dk, dv, dtail], axis=1)
        dx, dg_pre = norm_proj_bwd_x(dproj, s["x"], pre_norm_g[l][None], full["w_in"][l], dx, tl=256, name=f"norm_proj_bwd_x_{l}")
        grads["w_in"][l] = norm_proj_bwd_w(dproj, s["ht"], tl=min(L, 2048), tn=w_in.shape[2], name=f"norm_proj_bwd_w_{l}")
        grads["pre_norm_g"][l], grads["post_norm_g"][l], grads["b_glu"][l] = dg_pre[0], dg_post[0], db_glu[0]

    def column_blocks(g):
        return g.reshape(g.shape[0], N_DEV, -1).transpose(1, 0, 2)

    parts = [jnp.stack(grads["w_in"], axis=1)]
    parts += [jnp.stack([column_blocks(g) for g in grads[n]], axis=1) for n in COLUMN_SHARDED[1:]]
    parts.append(jnp.stack([g.reshape(N_DEV, -1, D) for g in grads["w_out"]], axis=1))
    received = exchange_blocks([p.astype(WIRE_DTYPE) for p in parts], name="exchange_weight_grads")
    results = {}
    for n, r in zip((*COLUMN_SHARDED, "w_out"), received):
        shard = weights[n]
        C = shard.shape[-1]
        as_rows = lambda a: a.reshape(-1, C)
        rows = as_rows(shard).shape[0]
        outs = adamw_from_parts(r.reshape(N_DEV, rows, C), as_rows(shard), as_rows(given["m_" + n]), as_rows(given["v_" + n]),
                                tr=_row_tile(rows), name=f"adamw_{n}")
        results[n] = [o.reshape(shard.shape) for o in outs]

    small = [weights[n] for n in SMALL]
    packed = _pack([jnp.stack(grads[n]) for n in SMALL])
    (everyones,) = all_gather([packed], name="gather_small_grads")
    outs = adamw_from_parts(everyones, _pack(small), _pack([given["m_" + n] for n in SMALL]), _pack([given["v_" + n] for n in SMALL]),
                            tr=PACK_ROWS, name="adamw_small")
    for n, *four in zip(SMALL, *(_unpack(o, small) for o in outs)):
        results[n] = four

    return (loss, dx[None], *(results[n][0] for n in WEIGHTS), *(results[n][1] for n in WEIGHTS),
            *(results[n][2] for n in WEIGHTS), *(results[n][3] for n in WEIGHTS))
```

```python
import functools

import jax
import jax.numpy as jnp
from jax import lax
from jax.experimental import pallas as pl
from jax.experimental.pallas import tpu as pltpu

F32 = jnp.float32
MXU_DTYPE = jnp.bfloat16
WIRE_DTYPE = jnp.bfloat16
ATTN_TILE = 256
ATTN_SPAN = 4

EPS = 1e-6
N_HEADS = 8
HEAD_DIM = 64
ADAM_LR = 0.001
ADAM_B1 = 0.9
ADAM_B2 = 0.999
ADAM_EPS = 1e-08
ADAM_WD = 0.01
ADAM_STEP = 10

N_DEV = 8
MESH_AXES = ("x", "y", "c")
VMEM_LIMIT_BYTES = 56 * 1024 * 1024
MESH = pl.DeviceIdType.MESH
ANY = pl.BlockSpec(memory_space=pl.ANY)
WHOLE_VMEM = pl.BlockSpec(memory_space=pltpu.VMEM)


def _params(n_grid_axes):
    return pltpu.CompilerParams(dimension_semantics=("arbitrary",) * n_grid_axes, vmem_limit_bytes=VMEM_LIMIT_BYTES)


def _dot(a, b):
    return jnp.dot(a.astype(MXU_DTYPE), b.astype(MXU_DTYPE), preferred_element_type=F32)


def _dot_nt(a, b):
    return lax.dot_general(a.astype(MXU_DTYPE), b.astype(MXU_DTYPE), (((1,), (1,)), ((), ())), preferred_element_type=F32)


def _dot_tn(a, b):
    return lax.dot_general(a.astype(MXU_DTYPE), b.astype(MXU_DTYPE), (((0,), (0,)), ((), ())), preferred_element_type=F32)


def _rms(v, g):
    return v * lax.rsqrt(jnp.mean(v * v, axis=-1, keepdims=True) + EPS) * g


def norm_proj(x, g, w, *, tl, tn, name):
    L, D = x.shape
    N = w.shape[1]

    def body(x_ref, g_ref, w_ref, o_ref, ht_ref, h_scr):
        @pl.when(pl.program_id(1) == 0)
        def _():
            h = _rms(x_ref[...], g_ref[...])
            h_scr[...] = h.astype(MXU_DTYPE)
            ht_ref[...] = h.T.astype(MXU_DTYPE)

        o_ref[...] = jnp.dot(h_scr[...], w_ref[...], preferred_element_type=F32)

    return pl.pallas_call(
        body, name=name, grid=(L // tl, N // tn),
        in_specs=[pl.BlockSpec((tl, D), lambda i, j: (i, 0)), pl.BlockSpec((1, D), lambda i, j: (0, 0)),
                  pl.BlockSpec((D, tn), lambda i, j: (0, j))],
        out_specs=[pl.BlockSpec((tl, tn), lambda i, j: (i, j)), pl.BlockSpec((D, tl), lambda i, j: (0, i))],
        out_shape=[jax.ShapeDtypeStruct((L, N), F32), jax.ShapeDtypeStruct((D, L), MXU_DTYPE)],
        scratch_shapes=[pltpu.VMEM((tl, D), MXU_DTYPE)],
        compiler_params=_params(2),
    )(x, g, w)


def norm_proj_bwd_x(dproj, x, g, w, dxn, *, tl, name):
    L, D = x.shape
    N = w.shape[1]

    def body(dp_ref, x_ref, g_ref, w_ref, dxn_ref, dx_ref, dg_ref):
        @pl.when(pl.program_id(0) == 0)
        def _():
            dg_ref[...] = jnp.zeros_like(dg_ref)

        dh = _dot_nt(dp_ref[...], w_ref[...])
        _, vjp = jax.vjp(_rms, x_ref[...], g_ref[...])
        dx, dg = vjp(dh)
        dx_ref[...] = dxn_ref[...] + dx
        dg_ref[...] += dg

    return pl.pallas_call(
        body, name=name, grid=(L // tl,),
        in_specs=[pl.BlockSpec((tl, N), lambda i: (i, 0)), pl.BlockSpec((tl, D), lambda i: (i, 0)),
                  pl.BlockSpec((1, D), lambda i: (0, 0)), WHOLE_VMEM, pl.BlockSpec((tl, D), lambda i: (i, 0))],
        out_specs=[pl.BlockSpec((tl, D), lambda i: (i, 0)), pl.BlockSpec((1, D), lambda i: (0, 0))],
        out_shape=[jax.ShapeDtypeStruct((L, D), F32), jax.ShapeDtypeStruct((1, D), F32)],
        compiler_params=_params(1),
    )(dproj, x, g, w, dxn)


def norm_proj_bwd_w(dproj, ht, *, tl, tn, name):
    D, L = ht.shape
    N = dproj.shape[1]

    def body(dp_ref, ht_ref, dw_ref):
        @pl.when(pl.program_id(1) == 0)
        def _():
            dw_ref[...] = jnp.zeros_like(dw_ref)

        dw_ref[...] += _dot(ht_ref[...], dp_ref[...])

    return pl.pallas_call(
        body, name=name, grid=(N // tn, L // tl),
        in_specs=[pl.BlockSpec((tl, tn), lambda j, i: (i, j)), pl.BlockSpec((D, tl), lambda j, i: (0, i))],
        out_specs=pl.BlockSpec((None, D, tn), lambda j, i: (j, 0, 0)),
        out_shape=jax.ShapeDtypeStruct((N // tn, D, tn), F32),
        compiler_params=_params(2),
    )(dproj, ht)


def _s5_disc(a_re, a_im, log_dt, b_re, b_im):
    dt = jnp.exp(log_dt)
    mag = jnp.exp(a_re * dt)
    abar_re = mag * jnp.cos(a_im * dt)
    abar_im = mag * jnp.sin(a_im * dt)
    nr = abar_re - 1.0
    ni = abar_im
    den = a_re * a_re + a_im * a_im
    f_re = (nr * a_re + ni * a_im) / den
    f_im = (ni * a_re - nr * a_im) / den
    bb_re = f_re[None] * b_re - f_im[None] * b_im
    bb_im = f_re[None] * b_im + f_im[None] * b_re
    return abar_re, abar_im, bb_re, bb_im


def s5_disc(a_re, a_im, log_dt, b_re, b_im, *, name):
    def body(a_re_ref, a_im_ref, dt_ref, b_re_ref, b_im_ref, *outs):
        vals = _s5_disc(a_re_ref[...], a_im_ref[...], dt_ref[...], b_re_ref[...], b_im_ref[...])
        for o_ref, val in zip(outs, vals):
            o_ref[...] = val

    sd = jax.ShapeDtypeStruct
    return pl.pallas_call(
        body, name=name,
        out_shape=[sd(a_re.shape, F32), sd(a_re.shape, F32), sd(b_re.shape, F32), sd(b_re.shape, F32)],
    )(a_re, a_im, log_dt, b_re, b_im)


def s5_disc_bwd(a_re, a_im, log_dt, b_re, b_im, d_abar_re, d_abar_im, d_bb_re, d_bb_im, *, name):
    def body(a_re_ref, a_im_ref, dt_ref, b_re_ref, b_im_ref, c0, c1, c2, c3, *outs):
        _, vjp = jax.vjp(_s5_disc, a_re_ref[...], a_im_ref[...], dt_ref[...], b_re_ref[...], b_im_ref[...])
        grads = vjp((c0[...], c1[...], c2[...], c3[...]))
        for o_ref, val in zip(outs, grads):
            o_ref[...] = val

    sd = jax.ShapeDtypeStruct
    return pl.pallas_call(
        body, name=name,
        out_shape=[sd(a_re.shape, F32), sd(a_re.shape, F32), sd(log_dt.shape, F32), sd(b_re.shape, F32), sd(b_re.shape, F32)],
    )(a_re, a_im, log_dt, b_re, b_im, d_abar_re, d_abar_im, d_bb_re, d_bb_im)


def _diag_halves(W, S):
    H, Q = W // 2, S // 2
    return [(slice(k * H, (k + 1) * H), (slice(k * Q, (k + 1) * Q), slice(S + k * Q, S + (k + 1) * Q))) for k in range(2)]


def s5_scan(proj, bb_bd, c_bd, abar_re, abar_im, d_skip, *, tl, name):
    L = proj.shape[0]
    W, S2 = bb_bd.shape
    S = S2 // 2

    def body(u_ref, bb_ref, c_ref, ar_ref, ai_ref, d_ref, y_ref, xs_ref, sr_scr, si_scr):
        @pl.when(pl.program_id(0) == 0)
        def _():
            sr_scr[...] = jnp.zeros_like(sr_scr)
            si_scr[...] = jnp.zeros_like(si_scr)

        u = u_ref[...]
        for rows, col_ranges in _diag_halves(W, S):
            for cols in col_ranges:
                xs_ref[:, cols] = _dot(u[:, rows], bb_ref[rows, cols])
        ar = ar_ref[...]
        ai = ai_ref[...]

        def step(t, carry):
            sr, si = carry
            row = pl.ds(t, 1)
            nsr = ar * sr - ai * si + xs_ref[row, pl.ds(0, S)]
            nsi = ar * si + ai * sr + xs_ref[row, pl.ds(S, S)]
            xs_ref[row, pl.ds(0, S)] = nsr
            xs_ref[row, pl.ds(S, S)] = nsi
            return nsr, nsi

        sr, si = lax.fori_loop(0, tl, step, (sr_scr[...], si_scr[...]), unroll=8)
        sr_scr[...] = sr
        si_scr[...] = si
        y = [sum(_dot(xs_ref[:, cols], c_ref[cols, rows]) for cols in col_ranges) for rows, col_ranges in _diag_halves(W, S)]
        y_ref[...] = jnp.concatenate(y, axis=1) + d_ref[...] * u

    return pl.pallas_call(
        body, name=name, grid=(L // tl,),
        in_specs=[pl.BlockSpec((tl, W), lambda i: (i, 0)), WHOLE_VMEM, WHOLE_VMEM, WHOLE_VMEM, WHOLE_VMEM, WHOLE_VMEM],
        out_specs=[pl.BlockSpec((tl, W), lambda i: (i, 0)), pl.BlockSpec((tl, S2), lambda i: (i, 0))],
        out_shape=[jax.ShapeDtypeStruct((L, W), F32), jax.ShapeDtypeStruct((L, S2), F32)],
        scratch_shapes=[pltpu.VMEM((1, S), F32), pltpu.VMEM((1, S), F32)],
        compiler_params=_params(1),
    )(proj, bb_bd, c_bd, abar_re, abar_im, d_skip)


def s5_scan_bwd(dy, proj, xs, bb_bd, c_bd, abar_re, abar_im, d_skip, *, tl, name):
    L = proj.shape[0]
    W, S2 = bb_bd.shape
    S = S2 // 2
    nt = L // tl
    HALO = 8

    def body(dy_ref, u_ref, xs_ref, xprev_ref, bb_ref, c_ref, ar_ref, ai_ref, d_ref,
             du_ref, dd_ref, dar_ref, dai_ref, dbb_hbm, dc_hbm, lam_scr, lr_scr, li_scr, dbb_scr, dc_scr, sem):
        i = pl.program_id(0)

        @pl.when(i == 0)
        def _():
            for ref in (lr_scr, li_scr, dbb_scr, dc_scr, dd_ref, dar_ref, dai_ref):
                ref[...] = jnp.zeros_like(ref)

        dy = dy_ref[...]
        u = u_ref[...]
        xs_v = xs_ref[...]
        halves = _diag_halves(W, S)
        for rows, col_ranges in halves:
            for cols in col_ranges:
                dc_scr[cols, rows] += _dot_tn(xs_ref[:, cols], dy[:, rows])
                lam_scr[:, cols] = _dot_nt(dy[:, rows], c_ref[cols, rows])
        ar = ar_ref[...]
        ai = ai_ref[...]

        def step(s, carry):
            lr, li = carry
            row = pl.ds(tl - 1 - s, 1)
            nlr = lam_scr[row, pl.ds(0, S)] + ar * lr + ai * li
            nli = lam_scr[row, pl.ds(S, S)] - ai * lr + ar * li
            lam_scr[row, pl.ds(0, S)] = nlr
            lam_scr[row, pl.ds(S, S)] = nli
            return nlr, nli

        lr, li = lax.fori_loop(0, tl, step, (lr_scr[...], li_scr[...]), unroll=8)
        lr_scr[...] = lr
        li_scr[...] = li
        lam = lam_scr[...]
        before_tile = jnp.where(i == nt - 1, 0.0, xprev_ref[pl.ds(HALO - 1, 1), :])
        first_row = lax.broadcasted_iota(jnp.int32, (tl, 1), 0) == 0
        x_prev = jnp.where(first_row, before_tile, pltpu.roll(xs_v, 1, 0))
        lam_r, lam_i = lam[:, :S], lam[:, S:]
        xp_r, xp_i = x_prev[:, :S], x_prev[:, S:]
        dar_ref[...] += jnp.sum(lam_r * xp_r + lam_i * xp_i, axis=0, keepdims=True)
        dai_ref[...] += jnp.sum(lam_i * xp_r - lam_r * xp_i, axis=0, keepdims=True)
        du = [sum(_dot_nt(lam_scr[:, cols], bb_ref[rows, cols]) for cols in col_ranges) for rows, col_ranges in halves]
        du_ref[...] = dy * d_ref[...] + jnp.concatenate(du, axis=1)
        for rows, col_ranges in halves:
            for cols in col_ranges:
                dbb_scr[rows, cols] += _dot_tn(u[:, rows], lam_scr[:, cols])
        dd_ref[...] += jnp.sum(dy * u, axis=0, keepdims=True)

        @pl.when(i == nt - 1)
        def _():
            for src, dst in ((dbb_scr, dbb_hbm), (dc_scr, dc_hbm)):
                cp = pltpu.make_async_copy(src, dst, sem)
                cp.start()
                cp.wait()

    rev = lambda i: (nt - 1 - i, 0)
    const = lambda i: (0, 0)
    halo = lambda i: (jnp.maximum((nt - 1 - i) * (tl // HALO) - 1, 0), 0)
    sd = jax.ShapeDtypeStruct
    return pl.pallas_call(
        body, name=name, grid=(nt,),
        in_specs=[pl.BlockSpec((tl, W), rev), pl.BlockSpec((tl, W), rev), pl.BlockSpec((tl, S2), rev),
                  pl.BlockSpec((HALO, S2), halo), WHOLE_VMEM, WHOLE_VMEM, WHOLE_VMEM, WHOLE_VMEM, WHOLE_VMEM],
        out_specs=[pl.BlockSpec((tl, W), rev), pl.BlockSpec((1, W), const), pl.BlockSpec((1, S), const),
                   pl.BlockSpec((1, S), const), ANY, ANY],
        out_shape=[sd((L, W), F32), sd((1, W), F32), sd((1, S), F32), sd((1, S), F32), sd((W, S2), F32), sd((S2, W), F32)],
        scratch_shapes=[pltpu.VMEM((tl, S2), F32), pltpu.VMEM((1, S), F32), pltpu.VMEM((1, S), F32),
                        pltpu.VMEM((W, S2), F32), pltpu.VMEM((S2, W), F32), pltpu.SemaphoreType.DMA],
        compiler_params=_params(1),
    )(dy, proj, xs, xs, bb_bd, c_bd, abar_re, abar_im, d_skip)


def _glu_gate(gl, z):
    half = gl.shape[1] // 2
    return gl[:, :half] * jax.nn.sigmoid(gl[:, half:]) * jax.nn.silu(z)


def glu_gate(y, proj, w, b, *, tl, name):
    L, W = y.shape

    def body(y_ref, z_ref, w_ref, b_ref, o_ref):
        gl = _dot(jax.nn.gelu(y_ref[...]), w_ref[...]) + b_ref[...]
        o_ref[...] = _glu_gate(gl, z_ref[...])

    return pl.pallas_call(
        body, name=name, grid=(L // tl,),
        in_specs=[pl.BlockSpec((tl, W), lambda i: (i, 0)), pl.BlockSpec((tl, W), lambda i: (i, 1)), WHOLE_VMEM, WHOLE_VMEM],
        out_specs=pl.BlockSpec((tl, W), lambda i: (i, 0)),
        out_shape=jax.ShapeDtypeStruct((L, W), F32),
        compiler_params=_params(1),
    )(y, proj, w, b)


def glu_gate_bwd(dys, y, proj, w, b, *, tl, name):
    L, W = y.shape

    def body(dys_ref, y_ref, z_ref, w_ref, b_ref, dy_ref, dz_ref, dw_ref, db_ref):
        @pl.when(pl.program_id(0) == 0)
        def _():
            dw_ref[...] = jnp.zeros_like(dw_ref)
            db_ref[...] = jnp.zeros_like(db_ref)

        yg, gelu_vjp = jax.vjp(jax.nn.gelu, y_ref[...])
        gl = _dot(yg, w_ref[...]) + b_ref[...]
        _, gate_vjp = jax.vjp(_glu_gate, gl, z_ref[...])
        dgl, dz = gate_vjp(dys_ref[...])
        dz_ref[...] = dz
        dy_ref[...] = gelu_vjp(_dot_nt(dgl, w_ref[...]))[0]
        dw_ref[...] += _dot_tn(yg, dgl)
        db_ref[...] += jnp.sum(dgl, axis=0, keepdims=True)

    row = lambda i: (i, 0)
    const = lambda i: (0, 0)
    sd = jax.ShapeDtypeStruct
    return pl.pallas_call(
        body, name=name, grid=(L // tl,),
        in_specs=[pl.BlockSpec((tl, W), row), pl.BlockSpec((tl, W), row), pl.BlockSpec((tl, W), lambda i: (i, 1)),
                  WHOLE_VMEM, WHOLE_VMEM],
        out_specs=[pl.BlockSpec((tl, W), row), pl.BlockSpec((tl, W), row), pl.BlockSpec((W, 2 * W), const),
                   pl.BlockSpec((1, 2 * W), const)],
        out_shape=[sd((L, W), F32), sd((L, W), F32), sd((W, 2 * W), F32), sd((1, 2 * W), F32)],
        compiler_params=_params(1),
    )(dys, y, proj, w, b)


def _logs(z, mask):
    t = jnp.minimum(z, 0.0)
    log_beta = t - jnp.log(1.0 + jnp.exp(t + t - z))
    log_1m = log_beta - z
    if mask is not None:
        log_1m = jnp.where(mask, log_1m, 0.0)
    hi = log_1m.astype(MXU_DTYPE)
    lo = (log_1m - hi.astype(F32)).astype(MXU_DTYPE)
    return log_beta, log_1m, jnp.concatenate([hi, lo], axis=1)


def _keys(t, tq):
    return pl.ds(pl.multiple_of(t * tq, tq), tq)


def _span(j, nk):
    return [j * nk + h for h in reversed(range(nk))]


def _prefetch(x_i, t_ref, j, nk, tq, scr):
    for h, t in enumerate(_span(jnp.maximum(j, 0), nk)):
        scr[h] = _dot(x_i, t_ref[:, _keys(t, tq)])


def columns_t(a, blocks, scales, *, width, tl, name):
    L = a.shape[0]
    n = len(blocks)

    def body(*refs):
        for a_ref, o_ref, s in zip(refs[:n], refs[n:], scales):
            o_ref[...] = (a_ref[...] * s).T.astype(MXU_DTYPE)

    return pl.pallas_call(
        body, name=name, grid=(L // tl,),
        in_specs=[pl.BlockSpec((tl, width), functools.partial(lambda b, i: (i, b), b)) for b in blocks],
        out_specs=[pl.BlockSpec((width, tl), lambda i: (0, i))] * n,
        out_shape=[jax.ShapeDtypeStruct((width, L), MXU_DTYPE)] * n,
        compiler_params=_params(1),
    )(*[a] * n)


def rows_from_t(arrays_t, *, tl, name):
    W, L = arrays_t[0].shape
    n = len(arrays_t)

    def body(*refs):
        for a_ref, o_ref in zip(refs[:n], refs[n:]):
            o_ref[...] = a_ref[...].T

    return pl.pallas_call(
        body, name=name, grid=(L // tl,),
        in_specs=[pl.BlockSpec((W, tl), lambda i: (0, i))] * n,
        out_specs=[pl.BlockSpec((tl, W), lambda i: (i, 0))] * n,
        out_shape=[jax.ShapeDtypeStruct((L, W), F32)] * n,
        compiler_params=_params(1),
    )(*arrays_t)


def _weights(z, masks, R, tri2, between):
    logs = [_logs(zv, m) for zv, m in zip(z, masks)]
    after = [jnp.dot(split, tri2, preferred_element_type=F32) for _, _, split in logs]
    between()
    w = []
    for (log_beta, log_1m, _), a, m in zip(logs, after, masks):
        wt = jnp.exp(log_beta + a + R)
        w.append(wt if m is None else jnp.where(m, wt, 0.0))
        R = R + jnp.sum(log_1m, axis=1, keepdims=True)
    return [lg[0] for lg in logs], w, R


def _iotas(t):
    return lax.broadcasted_iota(jnp.int32, (t, t), 0), lax.broadcasted_iota(jnp.int32, (t, t), 1)


def _suffix_tri2(row, col):
    tri = (row > col).astype(MXU_DTYPE)
    return jnp.concatenate([tri, tri], axis=0)


def _rows(t_ref):
    return t_ref[...].astype(F32).T.astype(MXU_DTYPE)


def attn_fwd(qt, kt, vt, *, tq, nk, name):
    nh, hd, L = qt.shape
    assert L % (tq * nk) == 0

    def body(qt_ref, kt_ref, vt_ref, o_ref, z_scr, w_scr):
        i = pl.program_id(1)
        q_i = _rows(qt_ref)
        row, col = _iotas(tq)
        tri2 = _suffix_tri2(row, col)
        causal = col < row
        J = i // nk

        def weights(j, R):
            z = [z_scr[h] for h in range(nk)]
            _, w, R = _weights(z, [None] * nk, R, tri2, lambda: _prefetch(q_i, kt_ref, j - 1, nk, tq, z_scr))
            for h, wt in enumerate(w):
                w_scr[h] = wt.astype(w_scr.dtype)
            return R

        def diagonal(n):
            def run(R):
                z = [_dot(q_i, kt_ref[:, _keys(i - h, tq)]) for h in range(n)]
                _, w, R = _weights(z, [causal] + [None] * (n - 1), R, tri2, lambda: _prefetch(q_i, kt_ref, J - 1, nk, tq, z_scr))
                for h in range(nk - n):
                    w_scr[h] = jnp.zeros((tq, tq), w_scr.dtype)
                for h, wt in enumerate(w):
                    w_scr[nk - n + h] = wt.astype(w_scr.dtype)
                return R
            return run

        def flush(j, acc):
            for h, t in enumerate(_span(j, nk)):
                acc = acc + _dot_nt(vt_ref[:, _keys(t, tq)], w_scr[h])
            return acc

        R = lax.switch(i % nk, [diagonal(n + 1) for n in range(nk)], jnp.zeros((tq, 1), F32))

        def step(s, carry):
            acc, R = carry
            j = J - 1 - s
            acc = flush(j + 1, acc)
            return acc, weights(j, R)

        acc, _ = lax.fori_loop(0, J, step, (jnp.zeros((hd, tq), F32), R))
        o_ref[...] = flush(0, acc)

    tile = pl.BlockSpec((None, hd, tq), lambda h, i: (h, 0, i))
    whole = pl.BlockSpec((None, hd, L), lambda h, i: (h, 0, 0))
    return pl.pallas_call(
        body, name=name, grid=(nh, L // tq),
        in_specs=[tile, whole, whole], out_specs=tile,
        out_shape=jax.ShapeDtypeStruct((nh, hd, L), F32),
        scratch_shapes=[pltpu.VMEM((nk, tq, tq), F32), pltpu.VMEM((nk, tq, tq), MXU_DTYPE)],
        compiler_params=_params(2),
    )(qt, kt, vt)


def attn_bwd(qt, kt, vt, dot, *, tq, nk, name):
    nh, hd, L = qt.shape
    assert L % (tq * nk) == 0
    nq = L // tq
    scale = hd ** -0.5

    def body(qt_ref, kt_ref, vt_ref, dot_ref, dq_ref, dk_ref, dv_ref, e_scr, beta_scr, z_scr, dw_scr, w_scr):
        i = pl.program_id(1)

        @pl.when(i == 0)
        def _():
            dk_ref[...] = jnp.zeros_like(dk_ref)
            dv_ref[...] = jnp.zeros_like(dv_ref)

        q_i = _rows(qt_ref)
        do_i = _rows(dot_ref)
        dot_i = dot_ref[...]
        qt_i = qt_ref[...]
        row, col = _iotas(tq)
        tri2 = _suffix_tri2(row, col)
        prefix_tri = (row < col).astype(MXU_DTYPE)
        causal = col < row
        J = i // nk

        def prefetch(j):
            _prefetch(q_i, kt_ref, j, nk, tq, z_scr)
            _prefetch(do_i, vt_ref, j, nk, tq, dw_scr)

        def keep(slot, t, log_beta, w, dw, mask):
            w_scr[slot] = w.astype(w_scr.dtype)
            e_scr[t] = dw * w
            beta = jnp.exp(log_beta)
            beta_scr[t] = (beta if mask is None else jnp.where(mask, beta, 0.0)).astype(beta_scr.dtype)

        def rebuild(j, R):
            dw = [dw_scr[h] for h in range(nk)]
            log_beta, w, R = _weights([z_scr[h] for h in range(nk)], [None] * nk, R, tri2, lambda: prefetch(j - 1))
            for h, t in enumerate(_span(j, nk)):
                keep(h, t, log_beta[h], w[h], dw[h], None)
            return R

        def diagonal(n):
            def run(R):
                z = [_dot(q_i, kt_ref[:, _keys(i - h, tq)]) for h in range(n)]
                dw = [_dot(do_i, vt_ref[:, _keys(i - h, tq)]) for h in range(n)]
                masks = [causal] + [None] * (n - 1)
                log_beta, w, R = _weights(z, masks, R, tri2, lambda: prefetch(J - 1))
                for h in range(nk - n):
                    t = i + nk - n - h
                    w_scr[h] = jnp.zeros((tq, tq), w_scr.dtype)
                    e_scr[t] = jnp.zeros((tq, tq), F32)
                    beta_scr[t] = jnp.zeros((tq, tq), beta_scr.dtype)
                for h in range(n):
                    keep(nk - n + h, i - h, log_beta[h], w[h], dw[h], masks[h])
                return R
            return run

        def flush(j):
            for h, t in enumerate(_span(j, nk)):
                dv_ref[:, _keys(t, tq)] += _dot(dot_i, w_scr[h])

        R = lax.switch(i % nk, [diagonal(n + 1) for n in range(nk)], jnp.zeros((tq, 1), F32))

        def step(s, R):
            j = J - 1 - s
            flush(j + 1)
            return rebuild(j, R)

        lax.fori_loop(0, J, step, R)
        flush(0)

        def push(first, n, carry):
            dq, P = carry
            tiles = [first + h for h in range(n)]
            e = [e_scr[t] for t in tiles]
            before = [_dot(ev, prefix_tri) for ev in e]
            for t, ev, bv in zip(tiles, e, before):
                dz = (ev - beta_scr[t].astype(F32) * (ev + bv + P)).astype(MXU_DTYPE)
                dk_ref[:, _keys(t, tq)] += _dot(qt_i, dz)
                dq = dq + _dot_nt(kt_ref[:, _keys(t, tq)], dz)
                P = P + jnp.sum(ev, axis=1, keepdims=True)
            return dq, P

        carry = lax.fori_loop(0, (J + 1) // 2, lambda m, c: push(m * 2 * nk, 2 * nk, c),
                              (jnp.zeros((hd, tq), F32), jnp.zeros((tq, 1), F32)))
        dq, _ = lax.cond(J % 2 == 0, lambda c: push(J * nk, nk, c), lambda c: c, carry)
        dq_ref[...] = dq * scale

    tile = pl.BlockSpec((None, hd, tq), lambda h, i: (h, 0, i))
    whole = pl.BlockSpec((None, hd, L), lambda h, i: (h, 0, 0))
    sd = jax.ShapeDtypeStruct((nh, hd, L), F32)
    return pl.pallas_call(
        body, name=name, grid=(nh, nq),
        in_specs=[tile, whole, whole, tile], out_specs=[tile, whole, whole],
        out_shape=[sd, sd, sd],
        scratch_shapes=[pltpu.VMEM((nq, tq, tq), F32), pltpu.VMEM((nq, tq, tq), MXU_DTYPE), pltpu.VMEM((nk, tq, tq), F32),
                        pltpu.VMEM((nk, tq, tq), F32), pltpu.VMEM((nk, tq, tq), MXU_DTYPE)],
        compiler_params=_params(2),
    )(qt, kt, vt, dot)


def _block_diag(blocks):
    G, m, n = blocks.shape
    eye = jnp.eye(G, dtype=blocks.dtype)
    return (blocks[:, :, None, :] * eye[:, None, :, None]).reshape(G * m, G * n)


def _diag_blocks(dense, m, n):
    G = dense.shape[0] // m
    return jnp.einsum("gmgn->gmn", dense.reshape(G, m, G, n))


def s5_prepare(a_re, a_im, log_dt, b_re, b_im, c_re, c_im, d_skip, *, name):
    G, P = a_re.shape
    b_re_t = b_re.transpose(2, 0, 1)
    b_im_t = b_im.transpose(2, 0, 1)
    log_dt = log_dt.reshape(G, 1)
    abar_re, abar_im, bb_re, bb_im = s5_disc(a_re, a_im, log_dt, b_re_t, b_im_t, name=name)
    bb_bd = jnp.concatenate([_block_diag(bb_re.transpose(1, 0, 2)), _block_diag(bb_im.transpose(1, 0, 2))], axis=1)
    c_bd = jnp.concatenate([_block_diag(c_re.transpose(0, 2, 1)), -_block_diag(c_im.transpose(0, 2, 1))], axis=0)
    return dict(a_re=a_re, a_im=a_im, log_dt=log_dt, b_re_t=b_re_t, b_im_t=b_im_t,
                abar_re=abar_re.reshape(1, G * P), abar_im=abar_im.reshape(1, G * P),
                bb_bd=bb_bd.astype(MXU_DTYPE), c_bd=c_bd.astype(MXU_DTYPE), d=d_skip.reshape(1, -1))


def s5_param_grads(prm, dd, dar, dai, dbb_bd, dc_bd, *, name):
    G, P = prm["a_re"].shape
    H = prm["b_re_t"].shape[0]
    S = G * P
    d_bb_re = _diag_blocks(dbb_bd[:, :S], H, P).transpose(1, 0, 2)
    d_bb_im = _diag_blocks(dbb_bd[:, S:], H, P).transpose(1, 0, 2)
    da_re, da_im, dlog_dt, db_re_t, db_im_t = s5_disc_bwd(
        prm["a_re"], prm["a_im"], prm["log_dt"], prm["b_re_t"], prm["b_im_t"],
        dar.reshape(G, P), dai.reshape(G, P), d_bb_re, d_bb_im, name=name)
    return dict(a_re=da_re, a_im=da_im, log_dt=dlog_dt.reshape(G), b_re=db_re_t.transpose(1, 2, 0), b_im=db_im_t.transpose(1, 2, 0),
                c_re=_diag_blocks(dc_bd[:S], P, H).transpose(0, 2, 1), c_im=-_diag_blocks(dc_bd[S:], P, H).transpose(0, 2, 1),
                d=dd.reshape(-1))


def _silu_gate(o, z):
    return o * jax.nn.silu(z)


def _merge(g_ssm, p_ssm, g_attn, p_attn):
    return jax.nn.sigmoid(g_ssm) * p_ssm + jax.nn.sigmoid(g_attn) * p_attn


def _proj_gates(tl, W, D):
    return [pl.BlockSpec((tl, W), lambda i: (i, 5)), pl.BlockSpec((tl, D), lambda i: (i, 3)), pl.BlockSpec((tl, D), lambda i: (i, 4))]


def merge_out(ys, o, proj, x, w_bs, w_ba, w_out, g_post, *, tl, name):
    L, W = ys.shape
    D = x.shape[1]

    def body(ys_ref, o_ref, z_ref, gs_ref, ga_ref, x_ref, wbs_ref, wba_ref, wout_ref, g_ref, xn_ref, out_ref):
        ya = _silu_gate(o_ref[...], z_ref[...])
        merged = _merge(gs_ref[...], _dot(ys_ref[...], wbs_ref[...]), ga_ref[...], _dot(ya, wba_ref[...]))
        out = _dot(merged, wout_ref[...])
        out_ref[...] = out
        xn_ref[...] = x_ref[...] + _rms(out, g_ref[...])

    row = lambda i: (i, 0)
    sd = jax.ShapeDtypeStruct
    return pl.pallas_call(
        body, name=name, grid=(L // tl,),
        in_specs=[pl.BlockSpec((tl, W), row), pl.BlockSpec((tl, W), row), *_proj_gates(tl, W, D), pl.BlockSpec((tl, D), row),
                  WHOLE_VMEM, WHOLE_VMEM, WHOLE_VMEM, WHOLE_VMEM],
        out_specs=[pl.BlockSpec((tl, D), row), pl.BlockSpec((tl, D), row)],
        out_shape=[sd((L, D), F32), sd((L, D), F32)],
        compiler_params=_params(1),
    )(ys, o, proj, proj, proj, x, w_bs, w_ba, w_out, g_post)


def merge_out_bwd(dxn, out, ys, o, proj, w_bs, w_ba, w_out, g_post, *, tl, name):
    L, W = ys.shape
    D = dxn.shape[1]

    def body(dxn_ref, out_ref, ys_ref, o_ref, z_ref, gs_ref, ga_ref, wbs_ref, wba_ref, wout_ref, g_ref,
             dys_ref, do_ref, dtail_ref, dwout_ref, dwbs_ref, dwba_ref, dg_ref):
        @pl.when(pl.program_id(0) == 0)
        def _():
            for ref in (dwout_ref, dwbs_ref, dwba_ref, dg_ref):
                ref[...] = jnp.zeros_like(ref)

        _, rms_vjp = jax.vjp(_rms, out_ref[...], g_ref[...])
        dout, dg = rms_vjp(dxn_ref[...])
        dg_ref[...] += dg
        ys_v = ys_ref[...]
        ya, gate_vjp = jax.vjp(_silu_gate, o_ref[...], z_ref[...])
        merged, merge_vjp = jax.vjp(_merge, gs_ref[...], _dot(ys_v, wbs_ref[...]), ga_ref[...], _dot(ya, wba_ref[...]))
        dwout_ref[...] += _dot_tn(merged, dout)
        dgs, dps, dga, dpa = merge_vjp(_dot_nt(dout, wout_ref[...]))
        dys_ref[...] = _dot_nt(dps, wbs_ref[...])
        dwbs_ref[...] += _dot_tn(ys_v, dps)
        dwba_ref[...] += _dot_tn(ya, dpa)
        do, dz = gate_vjp(_dot_nt(dpa, wba_ref[...]))
        do_ref[...] = do
        dtail_ref[:, pl.ds(0, W)] = dz
        dtail_ref[:, pl.ds(W, D)] = dgs
        dtail_ref[:, pl.ds(W + D, D)] = dga

    row = lambda i: (i, 0)
    const = lambda i: (0, 0)
    sd = jax.ShapeDtypeStruct
    return pl.pallas_call(
        body, name=name, grid=(L // tl,),
        in_specs=[pl.BlockSpec((tl, D), row), pl.BlockSpec((tl, D), row), pl.BlockSpec((tl, W), row), pl.BlockSpec((tl, W), row),
                  *_proj_gates(tl, W, D), WHOLE_VMEM, WHOLE_VMEM, WHOLE_VMEM, WHOLE_VMEM],
        out_specs=[pl.BlockSpec((tl, W), row), pl.BlockSpec((tl, W), row), pl.BlockSpec((tl, W + 2 * D), row),
                   pl.BlockSpec((D, D), const), pl.BlockSpec((W, D), const), pl.BlockSpec((W, D), const), pl.BlockSpec((1, D), const)],
        out_shape=[sd((L, W), F32), sd((L, W), F32), sd((L, W + 2 * D), F32), sd((D, D), F32), sd((W, D), F32), sd((W, D), F32),
                   sd((1, D), F32)],
        compiler_params=_params(1),
    )(dxn, out, ys, o, proj, proj, proj, w_bs, w_ba, w_out, g_post)


def loss_grad(y, target, *, tl, name):
    L, D = y.shape

    def body(y_ref, t_ref, dy_ref, loss_ref):
        @pl.when(pl.program_id(0) == 0)
        def _():
            loss_ref[...] = jnp.zeros_like(loss_ref)

        err = y_ref[...] - t_ref[...]
        dy_ref[...] = err / D
        loss_ref[...] += 0.5 * jnp.sum(jnp.mean(err * err, axis=-1, keepdims=True), axis=0, keepdims=True)

    row = lambda i: (i, 0)
    return pl.pallas_call(
        body, name=name, grid=(L // tl,),
        in_specs=[pl.BlockSpec((tl, D), row), pl.BlockSpec((tl, D), row)],
        out_specs=[pl.BlockSpec((tl, D), row), pl.BlockSpec((1, 1), lambda i: (0, 0))],
        out_shape=[jax.ShapeDtypeStruct((L, D), F32), jax.ShapeDtypeStruct((1, 1), F32)],
        compiler_params=_params(1),
    )(y, target)


def adamw_from_parts(parts, w, m, v, *, tr, name):
    n, R, C = parts.shape

    def body(p_ref, w_ref, m_ref, v_ref, g_ref, d_ref, nm_ref, nv_ref):
        g = p_ref[0].astype(F32)
        for s in range(1, n):
            g = g + p_ref[s].astype(F32)
        nm = ADAM_B1 * m_ref[...] + (1.0 - ADAM_B1) * g
        nv = ADAM_B2 * v_ref[...] + (1.0 - ADAM_B2) * (g * g)
        m_hat = nm / (1.0 - ADAM_B1 ** ADAM_STEP)
        v_hat = nv / (1.0 - ADAM_B2 ** ADAM_STEP)
        g_ref[...] = g
        d_ref[...] = -ADAM_LR * (m_hat / (jnp.sqrt(v_hat) + ADAM_EPS) + ADAM_WD * w_ref[...])
        nm_ref[...] = nm
        nv_ref[...] = nv

    tile = pl.BlockSpec((tr, C), lambda i: (i, 0))
    sd = jax.ShapeDtypeStruct((R, C), F32)
    return pl.pallas_call(
        body, name=name, grid=(R // tr,),
        in_specs=[pl.BlockSpec((n, tr, C), lambda i: (0, i, 0)), tile, tile, tile],
        out_specs=[tile, tile, tile, tile], out_shape=[sd, sd, sd, sd],
        compiler_params=_params(1),
    )(parts, w, m, v)


PER_PEER = N_DEV - 1


def _position():
    return lax.axis_index("x"), lax.axis_index("y"), lax.axis_index("c")


def _flat(x, y, c):
    return 4 * x + 2 * y + c


def all_gather(shards, *, name):
    n = len(shards)

    def body(*refs):
        ins, outs = refs[:n], refs[n:2 * n]
        send_sems, recv_sems, local_sems = refs[2 * n:]
        x, y, c = _position()
        me, sibling = (x, y, c), (x, y, 1 - c)
        chips = [(1 - x, y), (x, 1 - y), (1 - x, 1 - y)]

        def slot(a, dev):
            return outs[a].at[_flat(*dev)]

        def copy(a, k, block, to, src=None):
            return pltpu.make_async_remote_copy(
                src_ref=slot(a, block) if src is None else src, dst_ref=slot(a, block),
                send_sem=send_sems.at[a * PER_PEER + k], recv_sem=recv_sems.at[a * PER_PEER + k],
                device_id=to, device_id_type=MESH)

        mine = [pltpu.make_async_copy(ins[a], slot(a, me), local_sems.at[a]) for a in range(n)]
        for cp in mine:
            cp.start()
        first = []
        for a in range(n):
            first.append(copy(a, 0, me, sibling, src=ins[a]))
            first += [copy(a, 1 + j, me, (*chip, c), src=ins[a]) for j, chip in enumerate(chips)]
        for cp in first:
            cp.start()
        passed = []
        for j, chip in enumerate(chips):
            for a in range(n):
                copy(a, 1 + j, (*chip, c), me).wait_recv()
                passed.append(copy(a, 4 + j, (*chip, c), sibling))
                passed[-1].start()
        for a in range(n):
            copy(a, 0, sibling, me).wait_recv()
            for j, chip in enumerate(chips):
                copy(a, 4 + j, (*chip, 1 - c), me).wait_recv()
        for cp in first + passed:
            cp.wait_send()
        for cp in mine:
            cp.wait()

    return pl.pallas_call(
        body, name=name,
        in_specs=[ANY] * n, out_specs=[ANY] * n,
        out_shape=[jax.ShapeDtypeStruct((N_DEV, *s.shape), s.dtype) for s in shards],
        scratch_shapes=[pltpu.SemaphoreType.DMA((n * PER_PEER,)), pltpu.SemaphoreType.DMA((n * PER_PEER,)),
                        pltpu.SemaphoreType.DMA((n,))],
    )(*shards)


def exchange_blocks(parts, *, name):
    n = len(parts)

    def body(*refs):
        ins, outs = refs[:n], refs[n:2 * n]
        send_sems, recv_sems, local_sems = refs[2 * n:]
        x, y, c = _position()
        me = _flat(x, y, c)
        peers = []
        for k in range(1, N_DEV):
            peer = (1 - x if k & 4 else x, 1 - y if k & 2 else y, 1 - c if k & 1 else c)
            peers.append((k - 1, peer, _flat(*peer)))

        mine = [pltpu.make_async_copy(ins[a].at[me], outs[a].at[me], local_sems.at[a]) for a in range(n)]
        for cp in mine:
            cp.start()
        sends = [pltpu.make_async_remote_copy(
            src_ref=ins[a].at[peer_flat], dst_ref=outs[a].at[me],
            send_sem=send_sems.at[a * PER_PEER + k], recv_sem=recv_sems.at[a * PER_PEER + k],
            device_id=peer, device_id_type=MESH) for a in range(n) for k, peer, peer_flat in peers]
        for cp in sends:
            cp.start()
        for a in range(n):
            for k, peer, peer_flat in peers:
                pltpu.make_async_remote_copy(
                    src_ref=ins[a].at[me], dst_ref=outs[a].at[peer_flat],
                    send_sem=send_sems.at[a * PER_PEER + k], recv_sem=recv_sems.at[a * PER_PEER + k],
                    device_id=peer, device_id_type=MESH).wait_recv()
        for cp in sends:
            cp.wait_send()
        for cp in mine:
            cp.wait()

    return pl.pallas_call(
        body, name=name,
        in_specs=[ANY] * n, out_specs=[ANY] * n,
        out_shape=[jax.ShapeDtypeStruct(p.shape, p.dtype) for p in parts],
        scratch_shapes=[pltpu.SemaphoreType.DMA((n * PER_PEER,)), pltpu.SemaphoreType.DMA((n * PER_PEER,)),
                        pltpu.SemaphoreType.DMA((n,))],
    )(*parts)


LANES = 128
SMALL = ("pre_norm_g", "post_norm_g", "ssm_a_re", "ssm_a_im", "ssm_log_dt", "ssm_b_re", "ssm_b_im", "ssm_c_re", "ssm_c_im",
         "ssm_d", "b_glu")
COLUMN_SHARDED = ("w_in", "w_glu", "w_branch_ssm", "w_branch_attn")
WEIGHTS = ("pre_norm_g", "post_norm_g", "w_in", "ssm_a_re", "ssm_a_im", "ssm_log_dt", "ssm_b_re", "ssm_b_im", "ssm_c_re",
           "ssm_c_im", "ssm_d", "w_glu", "b_glu", "w_branch_ssm", "w_branch_attn", "w_out")
PACK_ROWS = 256


def _pack(arrays):
    flat = jnp.concatenate([a.reshape(-1) for a in arrays])
    tile = PACK_ROWS * LANES
    return jnp.pad(flat, (0, -flat.size % tile)).reshape(-1, LANES)


def _unpack(packed, like):
    flat, out, at = packed.reshape(-1), [], 0
    for a in like:
        out.append(flat[at:at + a.size].reshape(a.shape))
        at += a.size
    return out


def _row_tile(rows):
    return next(t for t in (256, 128, 64, 32, 16, 8) if rows % t == 0)


def kernel(x, pre_norm_g, post_norm_g, w_in, ssm_a_re, ssm_a_im, ssm_log_dt, ssm_b_re, ssm_b_im, ssm_c_re, ssm_c_im, ssm_d, w_glu, b_glu, w_branch_ssm, w_branch_attn, w_out, loss_target, m_pre_norm_g, m_post_norm_g, m_w_in, m_ssm_a_re, m_ssm_a_im, m_ssm_log_dt, m_ssm_b_re, m_ssm_b_im, m_ssm_c_re, m_ssm_c_im, m_ssm_d, m_w_glu, m_b_glu, m_w_branch_ssm, m_w_branch_attn, m_w_out, v_pre_norm_g, v_post_norm_g, v_w_in, v_ssm_a_re, v_ssm_a_im, v_ssm_log_dt, v_ssm_b_re, v_ssm_b_im, v_ssm_c_re, v_ssm_c_im, v_ssm_d, v_w_glu, v_b_glu, v_w_branch_ssm, v_w_branch_attn, v_w_out):
    given = dict(locals())
    weights = {n: given[n] for n in WEIGHTS}
    depth, D = pre_norm_g.shape
    h = x[0]
    L = h.shape[0]
    W = w_glu.shape[1]
    by_head = (N_HEADS, HEAD_DIM, L)

    gathered = all_gather([weights[n].astype(MXU_DTYPE) for n in (*COLUMN_SHARDED, "w_out")], name="gather_weights")
    full = {n: g.transpose(1, 2, 0, 3).reshape(depth, g.shape[2], -1) for n, g in zip(COLUMN_SHARDED, gathered)}
    full["w_out"] = gathered[-1].transpose(1, 0, 2, 3).reshape(depth, -1, D)

    saved = []
    for l in range(depth):
        prm = s5_prepare(ssm_a_re[l], ssm_a_im[l], ssm_log_dt[l], ssm_b_re[l], ssm_b_im[l], ssm_c_re[l], ssm_c_im[l], ssm_d[l],
                         name=f"s5_disc_{l}")
        proj, ht = norm_proj(h, pre_norm_g[l][None], full["w_in"][l], tl=min(L, 1024), tn=2560, name=f"norm_proj_{l}")
        y, states = s5_scan(proj, prm["bb_bd"], prm["c_bd"], prm["abar_re"], prm["abar_im"], prm["d"], tl=256, name=f"s5_scan_{l}")
        ys = glu_gate(y, proj, full["w_glu"][l], b_glu[l][None], tl=512, name=f"glu_gate_{l}")
        qt, kt, vt = (t.reshape(by_head) for t in columns_t(proj, (2, 3, 4), (HEAD_DIM ** -0.5, 1.0, 1.0), width=W, tl=512,
                                                            name=f"qkv_t_{l}"))
        (o,) = rows_from_t([attn_fwd(qt, kt, vt, tq=ATTN_TILE, nk=ATTN_SPAN, name=f"attn_fwd_{l}").reshape(W, L)], tl=512,
                           name=f"attn_out_{l}")
        h_next, out = merge_out(ys, o, proj, h, full["w_branch_ssm"][l], full["w_branch_attn"][l], full["w_out"][l],
                                post_norm_g[l][None], tl=256, name=f"merge_out_{l}")
        saved.append(dict(x=h, ht=ht, prm=prm, proj=proj, y=y, states=states, ys=ys, qt=qt, kt=kt, vt=vt, o=o, out=out))
        h = h_next

    dx, loss_part = loss_grad(h, loss_target[0], tl=512, name="loss_grad")
    loss = lax.psum(loss_part[0, 0], MESH_AXES)

    grads = {n: [None] * depth for n in WEIGHTS}
    for l in reversed(range(depth)):
        s = saved[l]
        prm = s["prm"]
        dys, do, dtail, grads["w_out"][l], grads["w_branch_ssm"][l], grads["w_branch_attn"][l], dg_post = merge_out_bwd(
            dx, s["out"], s["ys"], s["o"], s["proj"], full["w_branch_ssm"][l], full["w_branch_attn"][l], full["w_out"][l],
            post_norm_g[l][None], tl=256, name=f"merge_out_bwd_{l}")
        (do_t,) = columns_t(do, (0,), (1.0,), width=W, tl=512, name=f"do_t_{l}")
        dqkv_t = attn_bwd(s["qt"], s["kt"], s["vt"], do_t.reshape(by_head), tq=ATTN_TILE, nk=ATTN_SPAN, name=f"attn_bwd_{l}")
        dq, dk, dv = rows_from_t([t.reshape(W, L) for t in dqkv_t], tl=512, name=f"attn_grads_{l}")
        dy, dz_ssm, grads["w_glu"][l], db_glu = glu_gate_bwd(dys, s["y"], s["proj"], full["w_glu"][l], b_glu[l][None], tl=512,
                                                             name=f"glu_gate_bwd_{l}")
        du, *s5_outs = s5_scan_bwd(dy, s["proj"], s["states"], prm["bb_bd"], prm["c_bd"], prm["abar_re"], prm["abar_im"], prm["d"],
                                   tl=256, name=f"s5_scan_bwd_{l}")
        for n, g in s5_param_grads(prm, *s5_outs, name=f"s5_disc_bwd_{l}").items():
            grads["ssm_" + n][l] = g
        dproj = jnp.concatenate([du, dz_ssm, dq, dk, dv, dtail], axis=1)
        dx, dg_pre = norm_proj_bwd_x(dproj, s["x"], pre_norm_g[l][None], full["w_in"][l], dx, tl=256, name=f"norm_proj_bwd_x_{l}")
        grads["w_in"][l] = norm_proj_bwd_w(dproj, s["ht"], tl=min(L, 2048), tn=w_in.shape[2], name=f"norm_proj_bwd_w_{l}")
        grads["pre_norm_g"][l], grads["post_norm_g"][l], grads["b_glu"][l] = dg_pre[0], dg_post[0], db_glu[0]

    def column_blocks(g):
        return g.reshape(g.shape[0], N_DEV, -1).transpose(1, 0, 2)

    parts = [jnp.stack(grads["w_in"], axis=1)]
    parts += [jnp.stack([column_blocks(g) for g in grads[n]], axis=1) for n in COLUMN_SHARDED[1:]]
    parts.append(jnp.stack([g.reshape(N_DEV, -1, D) for g in grads["w_out"]], axis=1))
    received = exchange_blocks([p.astype(WIRE_DTYPE) for p in parts], name="exchange_weight_grads")
    results = {}
    for n, r in zip((*COLUMN_SHARDED, "w_out"), received):
        shard = weights[n]
        C = shard.shape[-1]
        as_rows = lambda a: a.reshape(-1, C)
        rows = as_rows(shard).shape[0]
        outs = adamw_from_parts(r.reshape(N_DEV, rows, C), as_rows(shard), as_rows(given["m_" + n]), as_rows(given["v_" + n]),
                                tr=_row_tile(rows), name=f"adamw_{n}")
        results[n] = [o.reshape(shard.shape) for o in outs]

    small = [weights[n] for n in SMALL]
    packed = _pack([jnp.stack(grads[n]) for n in SMALL])
    (everyones,) = all_gather([packed], name="gather_small_grads")
    outs = adamw_from_parts(everyones, _pack(small), _pack([given["m_" + n] for n in SMALL]), _pack([given["v_" + n] for n in SMALL]),
                            tr=PACK_ROWS, name="adamw_small")
    for n, *four in zip(SMALL, *(_unpack(o, small) for o in outs)):
        results[n] = four

    return (loss, dx[None], *(results[n][0] for n in WEIGHTS), *(results[n][1] for n in WEIGHTS),
            *(results[n][2] for n in WEIGHTS), *(results[n][3] for n in WEIGHTS))
```
